```python
import math
import jax, jax.numpy as jnp
from jax import lax
import numpy as np

D_MODEL = 1024
BATCH = 2
SEQ = 8192
DEPTH = 4

N_EVEN = (DEPTH + 1) // 2
N_ODD = DEPTH // 2
D_FF = 2816
N_MOD = 9
EPS = 1e-6

ATT_HEADS = 4
ATT_QK_DIM = 64
ATT_V_DIM = 2 * ATT_QK_DIM
QK_COLS = ATT_HEADS * 2 * ATT_QK_DIM
ATT_WIDTH = ATT_HEADS * ATT_V_DIM
Q_BLOCK = 128

POOL_WINDOWS = (2, 4, 8, 16)
POOL_GROUPS = len(POOL_WINDOWS)
POOL_WIDTH = D_MODEL // 2
POOL_GROUP_DIM = POOL_WIDTH // POOL_GROUPS

AB_IN_WIDTH = 2 * QK_COLS + ATT_WIDTH + POOL_WIDTH
MIX_WIDTH = ATT_WIDTH + POOL_WIDTH

CONV_WIDTH = D_MODEL
CONV_K = 3

kernel_name = "hybrid_diffattn_pool_shortconv_macaron_adaln"


def rmsnorm(x, g):
    xf = x.astype(jnp.float32)
    y = xf * lax.rsqrt(jnp.mean(xf * xf, axis=-1, keepdims=True) + EPS)
    return (y * g.astype(jnp.float32)).astype(x.dtype)


def modulated_norm(x, g, shift, scale):
    return rmsnorm(x, g) * (1 + scale) + shift


def swiglu(h, wg, wu, wd):
    return (jax.nn.silu(h @ wg) * (h @ wu)) @ wd


def diff_attention(q1, q2, k1, k2, v, lam):
    b, h, s, _ = q1.shape
    nb = s // Q_BLOCK
    scale = ATT_QK_DIM ** -0.5
    kpos = jnp.arange(s)

    def blocks(q):
        return q.reshape(b, h, nb, Q_BLOCK, q.shape[-1]).transpose(2, 0, 1, 3, 4)

    def one_block(args):
        qa, qb, i = args
        qpos = i * Q_BLOCK + jnp.arange(Q_BLOCK)
        mask = kpos[None, :] <= qpos[:, None]

        def probs(q, k):
            sc = jnp.einsum('bhqd,bhkd->bhqk', q, k).astype(jnp.float32) * scale
            sc = jnp.where(mask, sc, -jnp.inf)
            return jax.nn.softmax(sc, axis=-1)

        w = probs(qa, k1) - lam * probs(qb, k2)
        return jnp.einsum('bhqk,bhkd->bhqd', w.astype(v.dtype), v)

    out = lax.map(one_block, (blocks(q1), blocks(q2), jnp.arange(nb)))
    return out.transpose(1, 2, 0, 3, 4).reshape(b, h, s, v.shape[-1])


def multiscale_pool(u, pool_w, pool_scale):
    b, s, _ = u.shape
    uf = u.astype(jnp.float32).reshape(b, s, POOL_GROUPS, POOL_GROUP_DIM)
    cs = jnp.cumsum(uf, axis=1)
    t = jnp.arange(s)
    outs = []
    for g, w in enumerate(POOL_WINDOWS):
        csg = cs[:, :, g]
        lagged = jnp.pad(csg, ((0, 0), (w, 0), (0, 0)))[:, :s]
        cnt = jnp.minimum(t + 1, w).astype(jnp.float32)[None, :, None]
        outs.append((csg - lagged) / cnt - uf[:, :, g])
    d = jnp.stack(outs, axis=2)
    y = jnp.einsum('bsgc,gcd->bsgd', d, pool_w.astype(jnp.float32))
    y = y.reshape(b, s, POOL_WIDTH) * pool_scale.astype(jnp.float32)
    return y.astype(u.dtype)


def attn_pool_mixer(h, layer_idx, w_in, qk_g, lq1, lk1, lq2, lk2, subln_g, pool_w, pool_scale, w_out):
    b, s, _ = h.shape
    proj = h @ w_in
    q, k, v, u = jnp.split(proj, [QK_COLS, 2 * QK_COLS, 2 * QK_COLS + ATT_WIDTH], axis=-1)
    q = rmsnorm(q.reshape(b, s, ATT_HEADS, 2, ATT_QK_DIM), qk_g[0]).transpose(0, 2, 3, 1, 4)
    k = rmsnorm(k.reshape(b, s, ATT_HEADS, 2, ATT_QK_DIM), qk_g[1]).transpose(0, 2, 3, 1, 4)
    v = v.reshape(b, s, ATT_HEADS, ATT_V_DIM).transpose(0, 2, 1, 3)
    lam_init = 0.8 - 0.6 * math.exp(-0.3 * layer_idx)
    f32 = jnp.float32
    lam = (jnp.exp(jnp.sum(lq1.astype(f32) * lk1.astype(f32)))
           - jnp.exp(jnp.sum(lq2.astype(f32) * lk2.astype(f32))) + lam_init)
    o = diff_attention(q[:, :, 0], q[:, :, 1], k[:, :, 0], k[:, :, 1], v, lam)
    o = rmsnorm(o, subln_g) * (1 - lam_init)
    o = o.transpose(0, 2, 1, 3).reshape(b, s, ATT_WIDTH)
    p = multiscale_pool(u, pool_w, pool_scale)
    return jnp.concatenate([o, p], axis=-1) @ w_out


def short_conv_mixer(h, w_in, conv_w, w_out):
    b_gate, c_gate, xin = jnp.split(h @ w_in, 3, axis=-1)
    v = c_gate * xin
    s = v.shape[1]
    vp = jnp.pad(v, ((0, 0), (CONV_K - 1, 0), (0, 0)))
    y = conv_w[0] * vp[:, 0:s]
    for j in range(1, CONV_K):
        y = y + conv_w[j] * vp[:, j:j + s]
    return (b_gate * y) @ w_out


def setup_inputs(seed: int = 0) -> dict:
    key = jax.random.key(seed)
    ks = jax.random.split(key, 24)
    f32 = jnp.float32

    def nrm(k, shape, scale):
        return jax.random.normal(k, shape, f32) * scale

    D = D_MODEL
    return {
        "x": nrm(ks[0], (BATCH, SEQ, D), 1.0),
        "c": nrm(ks[1], (BATCH, D), 1.0),
        "norm_g": 1.0 + nrm(ks[2], (DEPTH, 3, D), 0.05),
        "w_ada": nrm(ks[3], (DEPTH, D, N_MOD * D), D ** -0.5),
        "b_ada": nrm(ks[4], (DEPTH, N_MOD * D), 0.02),
        "ffn_wg": nrm(ks[5], (DEPTH, 2, D, D_FF), D ** -0.5),
        "ffn_wu": nrm(ks[6], (DEPTH, 2, D, D_FF), D ** -0.5),
        "ffn_wd": nrm(ks[7], (DEPTH, 2, D_FF, D), D_FF ** -0.5),
        "w_in_ab": nrm(ks[8], (N_EVEN, D, AB_IN_WIDTH), D ** -0.5),
        "qk_norm_g": 1.0 + nrm(ks[9], (N_EVEN, 2, ATT_QK_DIM), 0.05),
        "lambda_q1": nrm(ks[10], (N_EVEN, ATT_QK_DIM), 0.1),
        "lambda_k1": nrm(ks[11], (N_EVEN, ATT_QK_DIM), 0.1),
        "lambda_q2": nrm(ks[12], (N_EVEN, ATT_QK_DIM), 0.1),
        "lambda_k2": nrm(ks[13], (N_EVEN, ATT_QK_DIM), 0.1),
        "subln_g": 1.0 + nrm(ks[14], (N_EVEN, ATT_V_DIM), 0.05),
        "pool_w": nrm(ks[15], (N_EVEN, POOL_GROUPS, POOL_GROUP_DIM, POOL_GROUP_DIM), POOL_GROUP_DIM ** -0.5),
        "pool_scale": 1.0 + nrm(ks[16], (N_EVEN, POOL_WIDTH), 0.1),
        "w_out_ab": nrm(ks[17], (N_EVEN, MIX_WIDTH, D), MIX_WIDTH ** -0.5),
        "w_in_c": nrm(ks[18], (N_ODD, D, 3 * CONV_WIDTH), D ** -0.5),
        "conv_w": nrm(ks[19], (N_ODD, CONV_K, CONV_WIDTH), CONV_K ** -0.5),
        "w_out_c": nrm(ks[20], (N_ODD, CONV_WIDTH, D), CONV_WIDTH ** -0.5),
    }


def reference(x, c, norm_g, w_ada, b_ada, ffn_wg, ffn_wu, ffn_wd, w_in_ab, qk_norm_g,
              lambda_q1, lambda_k1, lambda_q2, lambda_k2, subln_g, pool_w, pool_scale,
              w_out_ab, w_in_c, conv_w, w_out_c):
    b = x.shape[0]
    c_act = jax.nn.silu(c)
    for l in range(DEPTH):
        mod = (c_act @ w_ada[l] + b_ada[l]).reshape(b, N_MOD, 1, D_MODEL)
        sh1, sc1, g1, sh2, sc2, g2, sh3, sc3, g3 = [mod[:, i] for i in range(N_MOD)]
        h = modulated_norm(x, norm_g[l, 0], sh1, sc1)
        x = x + 0.5 * g1 * swiglu(h, ffn_wg[l, 0], ffn_wu[l, 0], ffn_wd[l, 0])
        h = modulated_norm(x, norm_g[l, 1], sh2, sc2)
        if l % 2 == 0:
            e = l // 2
            m = attn_pool_mixer(h, l, w_in_ab[e], qk_norm_g[e], lambda_q1[e], lambda_k1[e],
                                lambda_q2[e], lambda_k2[e], subln_g[e], pool_w[e],
                                pool_scale[e], w_out_ab[e])
        else:
            o = l // 2
            m = short_conv_mixer(h, w_in_c[o], conv_w[o], w_out_c[o])
        x = x + g2 * m
        h = modulated_norm(x, norm_g[l, 2], sh3, sc3)
        x = x + 0.5 * g3 * swiglu(h, ffn_wg[l, 1], ffn_wu[l, 1], ffn_wd[l, 1])
    return x
```

```python
import functools
import math

import jax
import jax.numpy as jnp
from jax import lax
from jax.experimental import pallas as pl
from jax.experimental.pallas import tpu as pltpu

F32 = jnp.float32
BF16 = jnp.bfloat16

EPS = 1e-6
N_MOD = 9
ATT_HEADS = 4
ATT_QK_DIM = 64
ATT_V_DIM = 128
QK_COLS = 512
ATT_WIDTH = 512
POOL_WINDOWS = (2, 4, 8, 16)
POOL_WIDTH = 512
POOL_GROUP_DIM = 128
CONV_K = 3

V7X_LANES = 128
V7X_MXU_DIM = 256
V7X_VMEM_BYTES = 64 * 1024 * 1024
VMEM_LIMIT_BYTES = 56 * 1024 * 1024

ROW_TILE = 512
ATT_TQ = 512
ATT_TK = 512
FF_CHUNK = V7X_MXU_DIM
POOL_HIST = 16
CONV_HIST = 8


def _params(n_axes):
    return pltpu.CompilerParams(
        dimension_semantics=("arbitrary",) * n_axes, vmem_limit_bytes=VMEM_LIMIT_BYTES)


def _const_spec(shape):
    nd = len(shape)
    return pl.BlockSpec(shape, lambda *_: (0,) * nd, pipeline_mode=pl.Buffered(1))


def _modulated_norm(x, g, scale, shift):
    ms = jnp.mean(x * x, axis=-1, keepdims=True)
    y = x * lax.rsqrt(ms + EPS)
    return (y * g) * (1.0 + scale) + shift


def _silu(x):
    return x * jax.nn.sigmoid(x)


def _ada_kernel(c_ref, w_ref, b_ref, o_ref):
    c_act = _silu(c_ref[...]).astype(BF16)
    w = w_ref[...].astype(BF16)
    o_ref[...] = jnp.dot(c_act, w, preferred_element_type=F32) + b_ref[...]


def _ada_modulation(c, w_ada, b_ada):
    depth, d, _ = w_ada.shape
    b = c.shape[0]
    rows = 8
    c_pad = jnp.zeros((rows, d), F32).at[:b].set(c)
    out = pl.pallas_call(
        _ada_kernel,
        grid=(depth, N_MOD),
        in_specs=[
            pl.BlockSpec((rows, d), lambda l, j: (0, 0)),
            pl.BlockSpec((None, d, d), lambda l, j: (l, 0, j)),
            pl.BlockSpec((None, None, 1, d), lambda l, j: (l, j, 0, 0)),
        ],
        out_specs=pl.BlockSpec((None, rows, d), lambda l, j: (l, 0, j)),
        out_shape=jax.ShapeDtypeStruct((depth, rows, N_MOD * d), F32),
        compiler_params=_params(2),
        name="ada_mod",
    )(c_pad, w_ada, b_ada.reshape(depth, N_MOD, 1, d))
    return out[:, :b].reshape(depth, b, N_MOD, d)


def _ffn_kernel(x_ref, mod_ref, ng_ref, wgu_ref, wd_ref, o_ref, a_ref, *, sub):
    x = x_ref[...]
    shift = mod_ref[3 * sub:3 * sub + 1, :]
    scale = mod_ref[3 * sub + 1:3 * sub + 2, :]
    gate = mod_ref[3 * sub + 2:3 * sub + 3, :]
    h = _modulated_norm(x, ng_ref[...], scale, shift).astype(BF16)
    d_ff = a_ref.shape[1]
    for c0 in range(0, d_ff, FF_CHUNK):
        gu = jnp.dot(h, wgu_ref[:, 2 * c0:2 * c0 + 2 * FF_CHUNK], preferred_element_type=F32)
        g = gu[:, :FF_CHUNK]
        u = gu[:, FF_CHUNK:]
        a_ref[:, c0:c0 + FF_CHUNK] = (_silu(g) * u).astype(BF16)
    y = jnp.dot(a_ref[...], wd_ref[...], preferred_element_type=F32)
    o_ref[...] = x + (0.5 * gate) * y


def _ffn(x, mod_l, ng, wgu, wd, sub):
    b, s, d = x.shape
    d_ff = wd.shape[0]
    tm = ROW_TILE
    row_spec = pl.BlockSpec((None, tm, d), lambda bi, i: (bi, i, 0))
    return pl.pallas_call(
        functools.partial(_ffn_kernel, sub=sub),
        grid=(b, s // tm),
        in_specs=[
            row_spec,
            pl.BlockSpec((None, N_MOD, d), lambda bi, i: (bi, 0, 0)),
            _const_spec((1, d)),
            _const_spec((d, 2 * d_ff)),
            _const_spec((d_ff, d)),
        ],
        out_specs=row_spec,
        out_shape=jax.ShapeDtypeStruct(x.shape, F32),
        scratch_shapes=[pltpu.VMEM((tm, d_ff), BF16)],
        compiler_params=_params(2),
        name=f"ffn{sub}",
    )(x, mod_l, ng, wgu, wd)


def _interleave_gate_up(wg, wu):
    lead = wg.shape[:-1]
    f = wg.shape[-1]
    g = wg.reshape(*lead, f // FF_CHUNK, 1, FF_CHUNK)
    u = wu.reshape(*lead, f // FF_CHUNK, 1, FF_CHUNK)
    return jnp.concatenate([g, u], axis=-2).reshape(*lead, 2 * f).astype(BF16)


def _group_mean_sq(xc, seg):
    sq = xc * xc
    hi = sq.astype(BF16)
    lo = (sq - hi.astype(F32)).astype(BF16)
    tot = jnp.dot(hi, seg, preferred_element_type=F32) + jnp.dot(lo, seg, preferred_element_type=F32)
    return tot * (1.0 / ATT_QK_DIM)


def _inproj_kernel(x_ref, mod_ref, ng_ref, win_ref, qkg_ref, seg_ref, pw_ref, ps_ref,
                   q_ref, k_ref, vt_ref, p_ref, proj_ref, uh_ref):
    t = pl.program_id(1)
    tm = x_ref.shape[0]
    x = x_ref[...]
    shift = mod_ref[3:4, :]
    scale = mod_ref[4:5, :]
    h = _modulated_norm(x, ng_ref[...], scale, shift).astype(BF16)
    proj_ref[...] = jnp.dot(h, win_ref[...], preferred_element_type=F32)

    seg = seg_ref[...]
    q_scale = ATT_QK_DIM ** -0.5
    for which, dst in ((0, q_ref), (1, k_ref)):
        gain = qkg_ref[which:which + 1, :]
        if which == 0:
            gain = gain * q_scale
        for c0 in range(0, QK_COLS, V7X_MXU_DIM):
            xc = proj_ref[:, which * QK_COLS + c0:which * QK_COLS + c0 + V7X_MXU_DIM]
            ms = _group_mean_sq(xc, seg)
            gain_c = gain[:, c0:c0 + V7X_MXU_DIM]
            dst[:, c0:c0 + V7X_MXU_DIM] = (xc * lax.rsqrt(ms + EPS) * gain_c).astype(BF16)

    v = proj_ref[:, 2 * QK_COLS:2 * QK_COLS + ATT_WIDTH]
    vt_ref[...] = v.T.astype(BF16)

    u0 = 2 * QK_COLS + ATT_WIDTH

    @pl.when(t == 0)
    def _():
        uh_ref[0:POOL_HIST, :] = jnp.zeros((POOL_HIST, POOL_WIDTH), F32)

    uh_ref[POOL_HIST:POOL_HIST + tm, :] = proj_ref[:, u0:u0 + POOL_WIDTH]
    pos = t * tm + lax.broadcasted_iota(jnp.int32, (tm, 1), 0)
    for g, w in enumerate(POOL_WINDOWS):
        cols = slice(g * POOL_GROUP_DIM, (g + 1) * POOL_GROUP_DIM)
        cur = uh_ref[POOL_HIST:POOL_HIST + tm, cols]
        acc = cur
        for j in range(1, w):
            acc = acc + uh_ref[POOL_HIST - j:POOL_HIST - j + tm, cols]
        cnt = jnp.minimum(pos + 1, w).astype(F32)
        dlt = (acc / cnt - cur).astype(BF16)
        y = jnp.dot(dlt, pw_ref[g], preferred_element_type=F32)
        p_ref[:, cols] = (y * ps_ref[:, cols]).astype(BF16)
    uh_ref[0:POOL_HIST, :] = uh_ref[tm:tm + POOL_HIST, :]


def _inproj(x, mod_l, ng, w_in, qk_gain, seg, pool_w, pool_scale):
    b, s, d = x.shape
    tm = ROW_TILE
    width = w_in.shape[1]
    row_in = pl.BlockSpec((None, tm, d), lambda bi, i: (bi, i, 0))
    row_out = pl.BlockSpec((None, tm, QK_COLS), lambda bi, i: (bi, i, 0))
    return pl.pallas_call(
        _inproj_kernel,
        grid=(b, s // tm),
        in_specs=[
            row_in,
            pl.BlockSpec((None, N_MOD, d), lambda bi, i: (bi, 0, 0)),
            _const_spec((1, d)),
            _const_spec((d, width)),
            _const_spec((2, QK_COLS)),
            _const_spec((V7X_MXU_DIM, V7X_MXU_DIM)),
            _const_spec(pool_w.shape),
            _const_spec((1, POOL_WIDTH)),
        ],
        out_specs=[
            row_out,
            row_out,
            pl.BlockSpec((None, None, ATT_WIDTH, tm), lambda bi, i: (bi, i, 0, 0)),
            row_out,
        ],
        out_shape=[
            jax.ShapeDtypeStruct((b, s, QK_COLS), BF16),
            jax.ShapeDtypeStruct((b, s, QK_COLS), BF16),
            jax.ShapeDtypeStruct((b, s // tm, ATT_WIDTH, tm), BF16),
            jax.ShapeDtypeStruct((b, s, POOL_WIDTH), BF16),
        ],
        scratch_shapes=[pltpu.VMEM((tm, width), F32), pltpu.VMEM((POOL_HIST + tm, POOL_WIDTH), F32)],
        compiler_params=_params(2),
        name="inproj_ab",
    )(x, mod_l, ng, w_in, qk_gain, seg, pool_w, pool_scale)


def _attn_kernel(q_ref, k_ref, vt_ref, lq1_ref, lk1_ref, lq2_ref, lk2_ref, sg_ref, o_ref,
                 m_ref, l_ref, acc_ref, *, lam_init):
    qi = pl.program_id(2)
    tq = q_ref.shape[0]
    tk = vt_ref.shape[2]
    q = q_ref[...]
    lane = lax.broadcasted_iota(jnp.int32, q.shape, 1)
    zero = jnp.zeros_like(q)
    q_maps = (jnp.where(lane < ATT_QK_DIM, q, zero), jnp.where(lane >= ATT_QK_DIM, q, zero))

    m_ref[...] = jnp.full(m_ref.shape, -jnp.inf, F32)
    l_ref[...] = jnp.zeros(l_ref.shape, F32)
    acc_ref[...] = jnp.zeros(acc_ref.shape, F32)

    def step(ki, masked):
        kb = k_ref[pl.ds(pl.multiple_of(ki * tk, tk), tk), :]
        vb = vt_ref[ki]
        for j in range(2):
            s = lax.dot_general(kb, q_maps[j], (((1,), (1,)), ((), ())), preferred_element_type=F32)
            if masked:
                kpos = lax.broadcasted_iota(jnp.int32, s.shape, 0)
                qpos = lax.broadcasted_iota(jnp.int32, s.shape, 1)
                s = jnp.where(kpos <= qpos, s, -jnp.inf)
            m_old = m_ref[j]
            m_new = jnp.maximum(m_old, jnp.max(s, axis=0, keepdims=True))
            alpha = jnp.exp(m_old - m_new)
            p = jnp.exp(s - m_new)
            l_ref[j] = alpha * l_ref[j] + jnp.sum(p, axis=0, keepdims=True)
            acc_ref[j] = alpha * acc_ref[j] + jnp.dot(vb, p.astype(BF16), preferred_element_type=F32)
            m_ref[j] = m_new

    def body(ki, carry):
        step(ki, False)
        return carry

    lax.fori_loop(0, qi, body, 0)
    step(qi, True)

    f32 = F32
    lam = (jnp.exp(jnp.sum(lq1_ref[...].astype(f32) * lk1_ref[...].astype(f32), axis=-1, keepdims=True))
           - jnp.exp(jnp.sum(lq2_ref[...].astype(f32) * lk2_ref[...].astype(f32), axis=-1, keepdims=True))
           + lam_init)
    o = acc_ref[0] / l_ref[0] - lam * (acc_ref[1] / l_ref[1])
    ms = jnp.mean(o * o, axis=0, keepdims=True)
    on = (o * lax.rsqrt(ms + EPS) * sg_ref[...]) * (1.0 - lam_init)
    o_ref[...] = on.T.astype(o_ref.dtype)


def _attention(q, k, vt, lq1, lk1, lq2, lk2, subln_g, lam_init):
    b, s, _ = q.shape
    nk, tk = vt.shape[1], vt.shape[3]
    tq = ATT_TQ
    assert tq == tk
    lam_spec = _const_spec((1, ATT_QK_DIM))
    return pl.pallas_call(
        functools.partial(_attn_kernel, lam_init=lam_init),
        grid=(b, ATT_HEADS, s // tq),
        in_specs=[
            pl.BlockSpec((None, tq, 2 * ATT_QK_DIM), lambda bi, h, i: (bi, i, h)),
            pl.BlockSpec((None, s, 2 * ATT_QK_DIM), lambda bi, h, i: (bi, 0, h)),
            pl.BlockSpec((None, nk, ATT_V_DIM, tk), lambda bi, h, i: (bi, 0, h, 0)),
            lam_spec, lam_spec, lam_spec, lam_spec,
            _const_spec((ATT_V_DIM, 1)),
        ],
        out_specs=pl.BlockSpec((None, tq, ATT_V_DIM), lambda bi, h, i: (bi, i, h)),
        out_shape=jax.ShapeDtypeStruct((b, s, ATT_WIDTH), BF16),
        scratch_shapes=[
            pltpu.VMEM((2, 1, tq), F32),
            pltpu.VMEM((2, 1, tq), F32),
            pltpu.VMEM((2, ATT_V_DIM, tq), F32),
        ],
        compiler_params=_params(3),
        name="diff_attn",
    )(q, k, vt, lq1, lk1, lq2, lk2, subln_g)


def _outproj_kernel(x_ref, mod_ref, o_ref, p_ref, w_ref, out_ref):
    gate = mod_ref[5:6, :]
    cat = jnp.concatenate([o_ref[...], p_ref[...]], axis=-1)
    out_ref[...] = x_ref[...] + gate * jnp.dot(cat, w_ref[...], preferred_element_type=F32)


def _outproj(x, mod_l, o, p, w_out):
    b, s, d = x.shape
    tm = ROW_TILE
    row = pl.BlockSpec((None, tm, d), lambda bi, i: (bi, i, 0))
    half = pl.BlockSpec((None, tm, ATT_WIDTH), lambda bi, i: (bi, i, 0))
    return pl.pallas_call(
        _outproj_kernel,
        grid=(b, s // tm),
        in_specs=[row, pl.BlockSpec((None, N_MOD, d), lambda bi, i: (bi, 0, 0)), half, half,
                  _const_spec(w_out.shape)],
        out_specs=row,
        out_shape=jax.ShapeDtypeStruct(x.shape, F32),
        compiler_params=_params(2),
        name="outproj_ab",
    )(x, mod_l, o, p, w_out)


def _conv_kernel(x_ref, mod_ref, ng_ref, win_ref, cw_ref, wout_ref, out_ref, proj_ref, vh_ref):
    t = pl.program_id(1)
    tm, d = x_ref.shape
    x = x_ref[...]
    shift = mod_ref[3:4, :]
    scale = mod_ref[4:5, :]
    gate = mod_ref[5:6, :]
    h = _modulated_norm(x, ng_ref[...], scale, shift).astype(BF16)
    proj_ref[...] = jnp.dot(h, win_ref[...], preferred_element_type=F32)

    @pl.when(t == 0)
    def _():
        vh_ref[0:CONV_HIST, :] = jnp.zeros((CONV_HIST, d), F32)

    vh_ref[CONV_HIST:CONV_HIST + tm, :] = proj_ref[:, d:2 * d] * proj_ref[:, 2 * d:3 * d]
    y = cw_ref[CONV_K - 1:CONV_K, :] * vh_ref[CONV_HIST:CONV_HIST + tm, :]
    for j in range(1, CONV_K):
        y = y + cw_ref[CONV_K - 1 - j:CONV_K - j, :] * vh_ref[CONV_HIST - j:CONV_HIST - j + tm, :]
    z = (proj_ref[:, 0:d] * y).astype(BF16)
    out_ref[...] = x + gate * jnp.dot(z, wout_ref[...], preferred_element_type=F32)
    vh_ref[0:CONV_HIST, :] = vh_ref[tm:tm + CONV_HIST, :]


def _conv_mixer(x, mod_l, ng, w_in, conv_w, w_out):
    b, s, d = x.shape
    tm = ROW_TILE
    row = pl.BlockSpec((None, tm, d), lambda bi, i: (bi, i, 0))
    return pl.pallas_call(
        _conv_kernel,
        grid=(b, s // tm),
        in_specs=[row, pl.BlockSpec((None, N_MOD, d), lambda bi, i: (bi, 0, 0)), _const_spec((1, d)),
                  _const_spec(w_in.shape), _const_spec(conv_w.shape), _const_spec(w_out.shape)],
        out_specs=row,
        out_shape=jax.ShapeDtypeStruct(x.shape, F32),
        scratch_shapes=[pltpu.VMEM((tm, 3 * d), F32), pltpu.VMEM((CONV_HIST + tm, d), F32)],
        compiler_params=_params(2),
        name="conv_mixer",
    )(x, mod_l, ng, w_in, conv_w, w_out)


def kernel(x, c, norm_g, w_ada, b_ada, ffn_wg, ffn_wu, ffn_wd, w_in_ab, qk_norm_g, lambda_q1, lambda_k1,
           lambda_q2, lambda_k2, subln_g, pool_w, pool_scale, w_out_ab, w_in_c, conv_w, w_out_c):
    depth = norm_g.shape[0]
    d = x.shape[-1]
    mod = _ada_modulation(c, w_ada, b_ada)

    wgu = _interleave_gate_up(ffn_wg, ffn_wu)
    wd = ffn_wd.astype(BF16)
    w_in_ab16 = w_in_ab.astype(BF16)
    w_out_ab16 = w_out_ab.astype(BF16)
    pool_w16 = pool_w.astype(BF16)
    w_in_c16 = w_in_c.astype(BF16)
    w_out_c16 = w_out_c.astype(BF16)
    grp = jnp.arange(V7X_MXU_DIM) // ATT_QK_DIM
    seg = (grp[:, None] == grp[None, :]).astype(BF16)
    qk_gain = jnp.tile(qk_norm_g, (1, 1, QK_COLS // ATT_QK_DIM))

    for l in range(depth):
        mod_l = mod[l]
        x = _ffn(x, mod_l, norm_g[l, 0].reshape(1, d), wgu[l, 0], wd[l, 0], sub=0)
        ng = norm_g[l, 1].reshape(1, d)
        if l % 2 == 0:
            e = l // 2
            lam_init = 0.8 - 0.6 * math.exp(-0.3 * l)
            q, k, vt, p = _inproj(x, mod_l, ng, w_in_ab16[e], qk_gain[e], seg, pool_w16[e],
                                  pool_scale[e].reshape(1, POOL_WIDTH))
            o = _attention(q, k, vt, lambda_q1[e].reshape(1, -1), lambda_k1[e].reshape(1, -1),
                           lambda_q2[e].reshape(1, -1), lambda_k2[e].reshape(1, -1),
                           subln_g[e].reshape(ATT_V_DIM, 1), lam_init)
            x = _outproj(x, mod_l, o, p, w_out_ab16[e])
        else:
            o_idx = l // 2
            x = _conv_mixer(x, mod_l, ng, w_in_c16[o_idx], conv_w[o_idx], w_out_c16[o_idx])
        x = _ffn(x, mod_l, norm_g[l, 2].reshape(1, d), wgu[l, 1], wd[l, 1], sub=2)
    return x
```

```python
import functools
import math

import jax
import jax.numpy as jnp
from jax import lax
from jax.experimental import pallas as pl
from jax.experimental.pallas import tpu as pltpu

F32 = jnp.float32
BF16 = jnp.bfloat16

EPS = 1e-6
N_MOD = 9
ATT_HEADS = 4
ATT_QK_DIM = 64
ATT_V_DIM = 128
QK_COLS = 512
ATT_WIDTH = 512
POOL_WINDOWS = (2, 4, 8, 16)
POOL_WIDTH = 512
POOL_GROUP_DIM = 128
CONV_K = 3

V7X_MXU_DIM = 256
VMEM_LIMIT_BYTES = 56 * 1024 * 1024

ROW_TILE = 512
ATT_TQ = 512
ATT_TK = ROW_TILE
ATT_QCOLS = V7X_MXU_DIM
FF_CHUNK = 2 * V7X_MXU_DIM
POOL_HIST = 16
CONV_HIST = 8
LOG2E = math.log2(math.e)


def _params(n_axes):
    return pltpu.CompilerParams(
        dimension_semantics=("arbitrary",) * n_axes, vmem_limit_bytes=VMEM_LIMIT_BYTES)


def _layer_spec(arr, *lead):
    tail = arr.shape[len(lead):]
    block = (None,) * len(lead) + tail
    index = tuple(lead) + (0,) * len(tail)
    return pl.BlockSpec(block, lambda *_: index, pipeline_mode=pl.Buffered(1))


def _modulated_norm(x, g, scale, shift):
    ms = jnp.mean(x * x, axis=-1, keepdims=True)
    y = x * lax.rsqrt(ms + EPS)
    return (y * g) * (1.0 + scale) + shift


def _silu(x):
    return x * jax.nn.sigmoid(x)


def _ada_kernel(c_ref, w_ref, b_ref, o_ref):
    c_act = _silu(c_ref[...]).astype(BF16)
    w = w_ref[...].astype(BF16)
    o_ref[...] = jnp.dot(c_act, w, preferred_element_type=F32) + b_ref[...]


def _ada_modulation(c, w_ada, b_ada):
    depth, d, _ = w_ada.shape
    b = c.shape[0]
    rows = 8
    c_pad = jnp.zeros((rows, d), F32).at[:b].set(c)
    out = pl.pallas_call(
        _ada_kernel,
        grid=(depth, N_MOD),
        in_specs=[
            pl.BlockSpec((rows, d), lambda l, j: (0, 0)),
            pl.BlockSpec((None, d, d), lambda l, j: (l, 0, j)),
            pl.BlockSpec((None, None, 1, d), lambda l, j: (l, j, 0, 0)),
        ],
        out_specs=pl.BlockSpec((None, rows, d), lambda l, j: (l, 0, j)),
        out_shape=jax.ShapeDtypeStruct((depth, rows, N_MOD * d), F32),
        compiler_params=_params(2),
        name="ada_mod",
    )(c_pad, w_ada, b_ada.reshape(depth, N_MOD, 1, d))
    return out[:, :b].reshape(depth, b, N_MOD, d)


def _mod_spec(mod, layer):
    d = mod.shape[-1]
    return pl.BlockSpec((None, None, N_MOD, d), lambda bi, i: (layer, bi, 0, 0))


def _ffn_kernel(x_ref, mod_ref, ng_ref, wg_ref, wu_ref, wd_ref, o_ref, a_ref, *, sub):
    x = x_ref[...]
    shift = mod_ref[3 * sub:3 * sub + 1, :]
    scale = mod_ref[3 * sub + 1:3 * sub + 2, :]
    gate = mod_ref[3 * sub + 2:3 * sub + 3, :]
    h = _modulated_norm(x, ng_ref[...], scale, shift).astype(BF16)
    d_ff = a_ref.shape[1]
    for c0 in range(0, d_ff, FF_CHUNK):
        c1 = min(c0 + FF_CHUNK, d_ff)
        g = jnp.dot(h, wg_ref[:, c0:c1], preferred_element_type=F32)
        u = jnp.dot(h, wu_ref[:, c0:c1], preferred_element_type=F32)
        a_ref[:, c0:c1] = (_silu(g) * u).astype(BF16)
    y = jnp.dot(a_ref[...], wd_ref[...], preferred_element_type=F32)
    o_ref[...] = x + (0.5 * gate) * y


def _ffn(x, mod, norm_g, wg, wu, wd, layer, half):
    b, s, d = x.shape
    d_ff = wd.shape[-2]
    tm = ROW_TILE
    sub = 2 * half
    row_spec = pl.BlockSpec((None, tm, d), lambda bi, i: (bi, i, 0))
    return pl.pallas_call(
        functools.partial(_ffn_kernel, sub=sub),
        grid=(b, s // tm),
        in_specs=[
            row_spec,
            _mod_spec(mod, layer),
            _layer_spec(norm_g, layer, sub),
            _layer_spec(wg, layer, half),
            _layer_spec(wu, layer, half),
            _layer_spec(wd, layer, half),
        ],
        out_specs=row_spec,
        out_shape=jax.ShapeDtypeStruct(x.shape, F32),
        scratch_shapes=[pltpu.VMEM((tm, d_ff), BF16)],
        compiler_params=_params(2),
        name=f"ffn{sub}",
    )(x, mod, norm_g, wg, wu, wd)


def _group_mean_sq(xc, seg):
    sq = xc * xc
    hi = sq.astype(BF16)
    lo = (sq - hi.astype(F32)).astype(BF16)
    tot = jnp.dot(hi, seg, preferred_element_type=F32) + jnp.dot(lo, seg, preferred_element_type=F32)
    return tot * (1.0 / ATT_QK_DIM)


def _inproj_kernel(x_ref, mod_ref, ng_ref, win_ref, qkg_ref, seg_ref, pw_ref, ps_ref,
                   q_ref, k_ref, vt_ref, p_ref, proj_ref, uh_ref):
    t = pl.program_id(1)
    tm = x_ref.shape[0]
    x = x_ref[...]
    shift = mod_ref[3:4, :]
    scale = mod_ref[4:5, :]
    h = _modulated_norm(x, ng_ref[...], scale, shift).astype(BF16)
    proj_ref[...] = jnp.dot(h, win_ref[...], preferred_element_type=F32)

    seg = seg_ref[...]
    q_scale = ATT_QK_DIM ** -0.5 * LOG2E
    for which, dst in ((0, q_ref), (1, k_ref)):
        gain = qkg_ref[which:which + 1, :]
        if which == 0:
            gain = gain * q_scale
        for c0 in range(0, QK_COLS, V7X_MXU_DIM):
            xc = proj_ref[:, which * QK_COLS + c0:which * QK_COLS + c0 + V7X_MXU_DIM]
            ms = _group_mean_sq(xc, seg)
            gain_c = gain[:, c0:c0 + V7X_MXU_DIM]
            dst[:, c0:c0 + V7X_MXU_DIM] = (xc * lax.rsqrt(ms + EPS) * gain_c).astype(BF16)

    v = proj_ref[:, 2 * QK_COLS:2 * QK_COLS + ATT_WIDTH]
    vt_ref[...] = v.T.astype(BF16)

    u0 = 2 * QK_COLS + ATT_WIDTH

    @pl.when(t == 0)
    def _():
        uh_ref[0:POOL_HIST, :] = jnp.zeros((POOL_HIST, POOL_WIDTH), F32)

    uh_ref[POOL_HIST:POOL_HIST + tm, :] = proj_ref[:, u0:u0 + POOL_WIDTH]
    pos = t * tm + lax.broadcasted_iota(jnp.int32, (tm, 1), 0)
    for g, w in enumerate(POOL_WINDOWS):
        cols = slice(g * POOL_GROUP_DIM, (g + 1) * POOL_GROUP_DIM)
        cur = uh_ref[POOL_HIST:POOL_HIST + tm, cols]
        acc = cur
        for j in range(1, w):
            acc = acc + uh_ref[POOL_HIST - j:POOL_HIST - j + tm, cols]
        cnt = jnp.minimum(pos + 1, w).astype(F32)
        dlt = (acc / cnt - cur).astype(BF16)
        y = jnp.dot(dlt, pw_ref[g], preferred_element_type=F32)
        p_ref[:, cols] = (y * ps_ref[:, cols]).astype(BF16)
    uh_ref[0:POOL_HIST, :] = uh_ref[tm:tm + POOL_HIST, :]


def _inproj(x, mod, norm_g, w_in, qk_gain, seg, pool_w, pool_scale, layer, e):
    b, s, d = x.shape
    tm = ROW_TILE
    width = w_in.shape[-1]
    row_in = pl.BlockSpec((None, tm, d), lambda bi, i: (bi, i, 0))
    row_out = pl.BlockSpec((None, tm, QK_COLS), lambda bi, i: (bi, i, 0))
    return pl.pallas_call(
        _inproj_kernel,
        grid=(b, s // tm),
        in_specs=[
            row_in,
            _mod_spec(mod, layer),
            _layer_spec(norm_g, layer, 1),
            _layer_spec(w_in, e),
            _layer_spec(qk_gain, e),
            _layer_spec(seg),
            _layer_spec(pool_w, e),
            _layer_spec(pool_scale, e),
        ],
        out_specs=[
            row_out,
            row_out,
            pl.BlockSpec((None, None, ATT_WIDTH, tm), lambda bi, i: (bi, i, 0, 0)),
            row_out,
        ],
        out_shape=[
            jax.ShapeDtypeStruct((b, s, QK_COLS), BF16),
            jax.ShapeDtypeStruct((b, s, QK_COLS), BF16),
            jax.ShapeDtypeStruct((b, s // tm, ATT_WIDTH, tm), BF16),
            jax.ShapeDtypeStruct((b, s, POOL_WIDTH), BF16),
        ],
        scratch_shapes=[pltpu.VMEM((tm, width), F32), pltpu.VMEM((POOL_HIST + tm, POOL_WIDTH), F32)],
        compiler_params=_params(2),
        name="inproj_ab",
    )(x, mod, norm_g, w_in, qk_gain, seg, pool_w, pool_scale)


def _attn_kernel(q_ref, k_ref, vt_ref, lq1_ref, lk1_ref, lq2_ref, lk2_ref, sg_ref, o_ref,
                 sa_ref, sb_ref, m_ref, l_ref, acc_ref, *, lam_init):
    qi = pl.program_id(2)
    tq = q_ref.shape[0]
    tk = vt_ref.shape[2]
    q = q_ref[...]
    lane = lax.broadcasted_iota(jnp.int32, q.shape, 1)
    zero = jnp.zeros_like(q)
    q_maps = (jnp.where(lane < ATT_QK_DIM, q, zero), jnp.where(lane >= ATT_QK_DIM, q, zero))

    m_ref[...] = jnp.full(m_ref.shape, -jnp.inf, F32)
    l_ref[...] = jnp.zeros(l_ref.shape, F32)
    acc_ref[...] = jnp.zeros(acc_ref.shape, F32)

    def scores(ki, dst_ref):
        kb = k_ref[pl.ds(pl.multiple_of(ki * tk, tk), tk), :]
        for j in range(2):
            dst_ref[j] = lax.dot_general(kb, q_maps[j], (((1,), (1,)), ((), ())),
                                         preferred_element_type=F32)

    def softmax_pv(ki, src_ref, masked):
        vb = vt_ref[ki]
        for j in range(2):
            for c0 in range(0, tq, ATT_QCOLS):
                cols = slice(c0, c0 + ATT_QCOLS)
                s = src_ref[j, :, cols]
                if masked:
                    kpos = lax.broadcasted_iota(jnp.int32, s.shape, 0)
                    qpos = lax.broadcasted_iota(jnp.int32, s.shape, 1) + c0
                    s = jnp.where(kpos <= qpos, s, -jnp.inf)
                m_old = m_ref[j, :, cols]
                m_new = jnp.maximum(m_old, jnp.max(s, axis=0, keepdims=True))
                alpha = jnp.exp2(m_old - m_new)
                p = jnp.exp2(s - m_new)
                l_ref[j, :, cols] = alpha * l_ref[j, :, cols] + jnp.sum(p, axis=0, keepdims=True)
                acc_ref[j, :, cols] = alpha * acc_ref[j, :, cols] + jnp.dot(
                    vb, p.astype(BF16), preferred_element_type=F32)
                m_ref[j, :, cols] = m_new

    scores(0, sa_ref)

    def pair(t, carry):
        k0 = 2 * t
        scores(k0 + 1, sb_ref)
        softmax_pv(k0, sa_ref, False)
        scores(k0 + 2, sa_ref)
        softmax_pv(k0 + 1, sb_ref, False)
        return carry

    lax.fori_loop(0, qi // 2, pair, 0)

    @pl.when(qi % 2 == 0)
    def _():
        softmax_pv(qi, sa_ref, True)

    @pl.when(qi % 2 == 1)
    def _():
        scores(qi, sb_ref)
        softmax_pv(qi - 1, sa_ref, False)
        softmax_pv(qi, sb_ref, True)

    lam = (jnp.exp(jnp.sum(lq1_ref[...] * lk1_ref[...], axis=-1, keepdims=True))
           - jnp.exp(jnp.sum(lq2_ref[...] * lk2_ref[...], axis=-1, keepdims=True))
           + lam_init)
    o = acc_ref[0] / l_ref[0] - lam * (acc_ref[1] / l_ref[1])
    ms = jnp.mean(o * o, axis=0, keepdims=True)
    on = (o * lax.rsqrt(ms + EPS) * sg_ref[...]) * (1.0 - lam_init)
    o_ref[...] = on.T.astype(o_ref.dtype)


def _attention(q, k, vt, lq1, lk1, lq2, lk2, subln_g, lam_init, e):
    b, s, _ = q.shape
    nk, tk = vt.shape[1], vt.shape[3]
    tq = ATT_TQ
    assert tq == tk
    return pl.pallas_call(
        functools.partial(_attn_kernel, lam_init=lam_init),
        grid=(b, ATT_HEADS, s // tq),
        in_specs=[
            pl.BlockSpec((None, tq, 2 * ATT_QK_DIM), lambda bi, h, i: (bi, i, h)),
            pl.BlockSpec((None, s, 2 * ATT_QK_DIM), lambda bi, h, i: (bi, 0, h)),
            pl.BlockSpec((None, nk, ATT_V_DIM, tk), lambda bi, h, i: (bi, 0, h, 0)),
            _layer_spec(lq1, e), _layer_spec(lk1, e), _layer_spec(lq2, e), _layer_spec(lk2, e),
            _layer_spec(subln_g, e),
        ],
        out_specs=pl.BlockSpec((None, tq, ATT_V_DIM), lambda bi, h, i: (bi, i, h)),
        out_shape=jax.ShapeDtypeStruct((b, s, ATT_WIDTH), BF16),
        scratch_shapes=[
            pltpu.VMEM((2, tk, tq), F32),
            pltpu.VMEM((2, tk, tq), F32),
            pltpu.VMEM((2, 1, tq), F32),
            pltpu.VMEM((2, 1, tq), F32),
            pltpu.VMEM((2, ATT_V_DIM, tq), F32),
        ],
        compiler_params=_params(3),
        name="diff_attn",
    )(q, k, vt, lq1, lk1, lq2, lk2, subln_g)


def _outproj_kernel(x_ref, mod_ref, o_ref, p_ref, w_ref, out_ref):
    gate = mod_ref[5:6, :]
    cat = jnp.concatenate([o_ref[...], p_ref[...]], axis=-1)
    out_ref[...] = x_ref[...] + gate * jnp.dot(cat, w_ref[...], preferred_element_type=F32)


def _outproj(x, mod, o, p, w_out, layer, e):
    b, s, d = x.shape
    tm = ROW_TILE
    row = pl.BlockSpec((None, tm, d), lambda bi, i: (bi, i, 0))
    half = pl.BlockSpec((None, tm, ATT_WIDTH), lambda bi, i: (bi, i, 0))
    return pl.pallas_call(
        _outproj_kernel,
        grid=(b, s // tm),
        in_specs=[row, _mod_spec(mod, layer), half, half, _layer_spec(w_out, e)],
        out_specs=row,
        out_shape=jax.ShapeDtypeStruct(x.shape, F32),
        compiler_params=_params(2),
        name="outproj_ab",
    )(x, mod, o, p, w_out)


def _conv_kernel(x_ref, mod_ref, ng_ref, win_ref, cw_ref, wout_ref, out_ref, proj_ref, vh_ref):
    t = pl.program_id(1)
    tm, d = x_ref.shape
    x = x_ref[...]
    shift = mod_ref[3:4, :]
    scale = mod_ref[4:5, :]
    gate = mod_ref[5:6, :]
    h = _modulated_norm(x, ng_ref[...], scale, shift).astype(BF16)
    proj_ref[...] = jnp.dot(h, win_ref[...], preferred_element_type=F32)

    @pl.when(t == 0)
    def _():
        vh_ref[0:CONV_HIST, :] = jnp.zeros((CONV_HIST, d), F32)

    vh_ref[CONV_HIST:CONV_HIST + tm, :] = proj_ref[:, d:2 * d] * proj_ref[:, 2 * d:3 * d]
    y = cw_ref[CONV_K - 1:CONV_K, :] * vh_ref[CONV_HIST:CONV_HIST + tm, :]
    for j in range(1, CONV_K):
        y = y + cw_ref[CONV_K - 1 - j:CONV_K - j, :] * vh_ref[CONV_HIST - j:CONV_HIST - j + tm, :]
    z = (proj_ref[:, 0:d] * y).astype(BF16)
    out_ref[...] = x + gate * jnp.dot(z, wout_ref[...], preferred_element_type=F32)
    vh_ref[0:CONV_HIST, :] = vh_ref[tm:tm + CONV_HIST, :]


def _conv_mixer(x, mod, norm_g, w_in, conv_w, w_out, layer, o_idx):
    b, s, d = x.shape
    tm = ROW_TILE
    row = pl.BlockSpec((None, tm, d), lambda bi, i: (bi, i, 0))
    return pl.pallas_call(
        _conv_kernel,
        grid=(b, s // tm),
        in_specs=[row, _mod_spec(mod, layer), _layer_spec(norm_g, layer, 1), _layer_spec(w_in, o_idx),
                  _layer_spec(conv_w, o_idx), _layer_spec(w_out, o_idx)],
        out_specs=row,
        out_shape=jax.ShapeDtypeStruct(x.shape, F32),
        scratch_shapes=[pltpu.VMEM((tm, 3 * d), F32), pltpu.VMEM((CONV_HIST + tm, d), F32)],
        compiler_params=_params(2),
        name="conv_mixer",
    )(x, mod, norm_g, w_in, conv_w, w_out)


def kernel(x, c, norm_g, w_ada, b_ada, ffn_wg, ffn_wu, ffn_wd, w_in_ab, qk_norm_g, lambda_q1, lambda_k1,
           lambda_q2, lambda_k2, subln_g, pool_w, pool_scale, w_out_ab, w_in_c, conv_w, w_out_c):
    depth = norm_g.shape[0]
    d = x.shape[-1]
    n_even = w_in_ab.shape[0]
    mod = _ada_modulation(c, w_ada, b_ada)

    wg = ffn_wg.astype(BF16)
    wu = ffn_wu.astype(BF16)
    wd = ffn_wd.astype(BF16)
    w_in_ab16 = w_in_ab.astype(BF16)
    w_out_ab16 = w_out_ab.astype(BF16)
    pool_w16 = pool_w.astype(BF16)
    w_in_c16 = w_in_c.astype(BF16)
    w_out_c16 = w_out_c.astype(BF16)
    grp = jnp.arange(V7X_MXU_DIM) // ATT_QK_DIM
    seg = (grp[:, None] == grp[None, :]).astype(BF16)
    qk_gain = jnp.tile(qk_norm_g, (1, 1, QK_COLS // ATT_QK_DIM))
    norm_g4 = norm_g.reshape(depth, 3, 1, d)
    pool_scale3 = pool_scale.reshape(n_even, 1, POOL_WIDTH)
    lam_rows = [a.reshape(n_even, 1, ATT_QK_DIM) for a in (lambda_q1, lambda_k1, lambda_q2, lambda_k2)]
    subln_col = subln_g.reshape(n_even, ATT_V_DIM, 1)

    for l in range(depth):
        x = _ffn(x, mod, norm_g4, wg, wu, wd, l, 0)
        if l % 2 == 0:
            e = l // 2
            lam_init = 0.8 - 0.6 * math.exp(-0.3 * l)
            q, k, vt, p = _inproj(x, mod, norm_g4, w_in_ab16, qk_gain, seg, pool_w16, pool_scale3, l, e)
            o = _attention(q, k, vt, *lam_rows, subln_col, lam_init, e)
            x = _outproj(x, mod, o, p, w_out_ab16, l, e)
        else:
            x = _conv_mixer(x, mod, norm_g4, w_in_c16, conv_w, w_out_c16, l, l // 2)
        x = _ffn(x, mod, norm_g4, wg, wu, wd, l, 1)
    return x
```

```python
import functools
import math

import jax
import jax.numpy as jnp
from jax import lax
from jax.experimental import pallas as pl
from jax.experimental.pallas import tpu as pltpu

F32 = jnp.float32
BF16 = jnp.bfloat16

EPS = 1e-6
N_MOD = 9
ATT_HEADS = 4
ATT_QK_DIM = 64
ATT_V_DIM = 128
QK_COLS = 512
ATT_WIDTH = 512
POOL_WINDOWS = (2, 4, 8, 16)
POOL_WIDTH = 512
POOL_GROUP_DIM = 128
CONV_K = 3

V7X_MXU_DIM = 256
VMEM_LIMIT_BYTES = 56 * 1024 * 1024

ROW_TILE = 512
FFN_TILE = 1024
ATT_QCOLS = V7X_MXU_DIM
FF_CHUNK = 2 * V7X_MXU_DIM
POOL_HIST = 16
CONV_HIST = 8
LOG2E = math.log2(math.e)


def _params(n_axes):
    return pltpu.CompilerParams(
        dimension_semantics=("arbitrary",) * n_axes, vmem_limit_bytes=VMEM_LIMIT_BYTES)


def _layer_spec(arr, *lead):
    tail = arr.shape[len(lead):]
    block = (None,) * len(lead) + tail
    index = tuple(lead) + (0,) * len(tail)
    return pl.BlockSpec(block, lambda *_: index, pipeline_mode=pl.Buffered(1))


def _modulated_norm(x, g, scale, shift):
    ms = jnp.mean(x * x, axis=-1, keepdims=True)
    y = x * lax.rsqrt(ms + EPS)
    return (y * g) * (1.0 + scale) + shift


def _silu(x):
    return x * jax.nn.sigmoid(x)


def _ada_kernel(c_ref, w_ref, b_ref, o_ref):
    c_act = _silu(c_ref[...]).astype(BF16)
    w = w_ref[...].astype(BF16)
    o_ref[...] = jnp.dot(c_act, w, preferred_element_type=F32) + b_ref[...]


def _ada_modulation(c, w_ada, b_ada):
    depth, d, _ = w_ada.shape
    b = c.shape[0]
    rows = 8
    c_pad = jnp.zeros((rows, d), F32).at[:b].set(c)
    out = pl.pallas_call(
        _ada_kernel,
        grid=(depth, N_MOD),
        in_specs=[
            pl.BlockSpec((rows, d), lambda l, j: (0, 0)),
            pl.BlockSpec((None, d, d), lambda l, j: (l, 0, j)),
            pl.BlockSpec((None, None, 1, d), lambda l, j: (l, j, 0, 0)),
        ],
        out_specs=pl.BlockSpec((None, rows, d), lambda l, j: (l, 0, j)),
        out_shape=jax.ShapeDtypeStruct((depth, rows, N_MOD * d), F32),
        compiler_params=_params(2),
        name="ada_mod",
    )(c_pad, w_ada, b_ada.reshape(depth, N_MOD, 1, d))
    return out[:, :b].reshape(depth, b, N_MOD, d)


def _mod_spec(mod, layer):
    d = mod.shape[-1]
    return pl.BlockSpec((None, None, N_MOD, d), lambda bi, i: (layer, bi, 0, 0))


def _ffn_kernel(x_ref, mod_ref, ng_ref, wg_ref, wu_ref, wd_ref, *rest, sub, mix):
    if mix:
        ao_ref, po_ref, wo_ref, o_ref, a_ref = rest
        cat = jnp.concatenate([ao_ref[...], po_ref[...]], axis=-1)
        x = x_ref[...] + mod_ref[5:6, :] * jnp.dot(cat, wo_ref[...], preferred_element_type=F32)
    else:
        o_ref, a_ref = rest
        x = x_ref[...]
    shift = mod_ref[3 * sub:3 * sub + 1, :]
    scale = mod_ref[3 * sub + 1:3 * sub + 2, :]
    gate = mod_ref[3 * sub + 2:3 * sub + 3, :]
    h = _modulated_norm(x, ng_ref[...], scale, shift).astype(BF16)
    d_ff = a_ref.shape[1]
    for c0 in range(0, d_ff, FF_CHUNK):
        c1 = min(c0 + FF_CHUNK, d_ff)
        g = jnp.dot(h, wg_ref[:, c0:c1], preferred_element_type=F32)
        u = jnp.dot(h, wu_ref[:, c0:c1], preferred_element_type=F32)
        a_ref[:, c0:c1] = (_silu(g) * u).astype(BF16)
    y = jnp.dot(a_ref[...], wd_ref[...], preferred_element_type=F32)
    o_ref[...] = x + (0.5 * gate) * y


def _ffn(x, mod, norm_g, wg, wu, wd, layer, half, mixer=None):
    b, s, d = x.shape
    d_ff = wd.shape[-2]
    tm = FFN_TILE
    sub = 2 * half
    row_spec = pl.BlockSpec((None, tm, d), lambda bi, i: (bi, i, 0))
    in_specs = [
        row_spec,
        _mod_spec(mod, layer),
        _layer_spec(norm_g, layer, sub),
        _layer_spec(wg, layer, half),
        _layer_spec(wu, layer, half),
        _layer_spec(wd, layer, half),
    ]
    args = [x, mod, norm_g, wg, wu, wd]
    if mixer is not None:
        ao, po, w_out, e = mixer
        half_spec = pl.BlockSpec((None, tm, ATT_WIDTH), lambda bi, i: (bi, i, 0))
        in_specs += [half_spec, half_spec, _layer_spec(w_out, e)]
        args += [ao, po, w_out]
    return pl.pallas_call(
        functools.partial(_ffn_kernel, sub=sub, mix=mixer is not None),
        grid=(b, s // tm),
        in_specs=in_specs,
        out_specs=row_spec,
        out_shape=jax.ShapeDtypeStruct(x.shape, F32),
        scratch_shapes=[pltpu.VMEM((tm, d_ff), BF16)],
        compiler_params=_params(2),
        name=f"ffn{sub}" + ("_mix" if mixer is not None else ""),
    )(*args)


def _group_mean_sq(xc, seg):
    sq = xc * xc
    hi = sq.astype(BF16)
    lo = (sq - hi.astype(F32)).astype(BF16)
    tot = jnp.dot(hi, seg, preferred_element_type=F32) + jnp.dot(lo, seg, preferred_element_type=F32)
    return tot * (1.0 / ATT_QK_DIM)


def _inproj_kernel(x_ref, mod_ref, ng_ref, win_ref, qkg_ref, seg_ref, pw_ref, ps_ref,
                   qt_ref, k_ref, vt_ref, p_ref, proj_ref, uh_ref):
    t = pl.program_id(1)
    tm = x_ref.shape[0]
    x = x_ref[...]
    shift = mod_ref[3:4, :]
    scale = mod_ref[4:5, :]
    h = _modulated_norm(x, ng_ref[...], scale, shift).astype(BF16)
    proj_ref[...] = jnp.dot(h, win_ref[...], preferred_element_type=F32)

    seg = seg_ref[...]
    q_scale = ATT_QK_DIM ** -0.5 * LOG2E
    for which in range(2):
        gain = qkg_ref[which:which + 1, :]
        if which == 0:
            gain = gain * q_scale
        for c0 in range(0, QK_COLS, V7X_MXU_DIM):
            xc = proj_ref[:, which * QK_COLS + c0:which * QK_COLS + c0 + V7X_MXU_DIM]
            ms = _group_mean_sq(xc, seg)
            normed = xc * lax.rsqrt(ms + EPS) * gain[:, c0:c0 + V7X_MXU_DIM]
            if which == 0:
                qt_ref[c0:c0 + V7X_MXU_DIM, :] = normed.T.astype(BF16)
            else:
                k_ref[:, c0:c0 + V7X_MXU_DIM] = normed.astype(BF16)

    v = proj_ref[:, 2 * QK_COLS:2 * QK_COLS + ATT_WIDTH]
    vt_ref[...] = v.T.astype(BF16)

    u0 = 2 * QK_COLS + ATT_WIDTH

    @pl.when(t == 0)
    def _():
        uh_ref[0:POOL_HIST, :] = jnp.zeros((POOL_HIST, POOL_WIDTH), F32)

    uh_ref[POOL_HIST:POOL_HIST + tm, :] = proj_ref[:, u0:u0 + POOL_WIDTH]
    pos = t * tm + lax.broadcasted_iota(jnp.int32, (tm, 1), 0)
    for g, w in enumerate(POOL_WINDOWS):
        cols = slice(g * POOL_GROUP_DIM, (g + 1) * POOL_GROUP_DIM)
        cur = uh_ref[POOL_HIST:POOL_HIST + tm, cols]
        acc = cur
        for j in range(1, w):
            acc = acc + uh_ref[POOL_HIST - j:POOL_HIST - j + tm, cols]
        cnt = jnp.minimum(pos + 1, w).astype(F32)
        dlt = (acc / cnt - cur).astype(BF16)
        y = jnp.dot(dlt, pw_ref[g], preferred_element_type=F32)
        p_ref[:, cols] = (y * ps_ref[:, cols]).astype(BF16)
    uh_ref[0:POOL_HIST, :] = uh_ref[tm:tm + POOL_HIST, :]


def _inproj(x, mod, norm_g, w_in, qk_gain, seg, pool_w, pool_scale, layer, e):
    b, s, d = x.shape
    tm = ROW_TILE
    width = w_in.shape[-1]
    row_in = pl.BlockSpec((None, tm, d), lambda bi, i: (bi, i, 0))
    row_out = pl.BlockSpec((None, tm, QK_COLS), lambda bi, i: (bi, i, 0))
    return pl.pallas_call(
        _inproj_kernel,
        grid=(b, s // tm),
        in_specs=[
            row_in,
            _mod_spec(mod, layer),
            _layer_spec(norm_g, layer, 1),
            _layer_spec(w_in, e),
            _layer_spec(qk_gain, e),
            _layer_spec(seg),
            _layer_spec(pool_w, e),
            _layer_spec(pool_scale, e),
        ],
        out_specs=[
            pl.BlockSpec((None, None, QK_COLS, tm), lambda bi, i: (bi, i, 0, 0)),
            row_out,
            pl.BlockSpec((None, None, ATT_WIDTH, tm), lambda bi, i: (bi, i, 0, 0)),
            row_out,
        ],
        out_shape=[
            jax.ShapeDtypeStruct((b, s // tm, QK_COLS, tm), BF16),
            jax.ShapeDtypeStruct((b, s, QK_COLS), BF16),
            jax.ShapeDtypeStruct((b, s // tm, ATT_WIDTH, tm), BF16),
            jax.ShapeDtypeStruct((b, s, POOL_WIDTH), BF16),
        ],
        scratch_shapes=[pltpu.VMEM((tm, width), F32), pltpu.VMEM((POOL_HIST + tm, POOL_WIDTH), F32)],
        compiler_params=_params(2),
        name="inproj_ab",
    )(x, mod, norm_g, w_in, qk_gain, seg, pool_w, pool_scale)


def _attn_kernel(qt_ref, k_ref, vt_ref, lq1_ref, lk1_ref, lq2_ref, lk2_ref, sg_ref, o_ref,
                 sa_ref, sb_ref, bma_ref, bmb_ref, m_ref, l_ref, acc_ref, *, lam_init):
    qi = pl.program_id(2)
    tq = qt_ref.shape[1]
    tk = vt_ref.shape[2]
    qt = qt_ref[...]
    feat = lax.broadcasted_iota(jnp.int32, qt.shape, 0)
    zero = jnp.zeros_like(qt)
    q_maps = (jnp.where(feat < ATT_QK_DIM, qt, zero), jnp.where(feat >= ATT_QK_DIM, qt, zero))

    m_ref[...] = jnp.full(m_ref.shape, -jnp.inf, F32)
    l_ref[...] = jnp.zeros(l_ref.shape, F32)
    acc_ref[...] = jnp.zeros(acc_ref.shape, F32)

    def scores(ki, buf, diagonal):
        dst_ref, bm_ref = buf
        kb = k_ref[pl.ds(pl.multiple_of(ki * tk, tk), tk), :]
        for j in range(2):
            s = jnp.dot(kb, q_maps[j], preferred_element_type=F32)
            if diagonal:
                kpos = lax.broadcasted_iota(jnp.int32, s.shape, 0)
                qpos = lax.broadcasted_iota(jnp.int32, s.shape, 1)
                s = jnp.where(kpos <= qpos, s, -jnp.inf)
            dst_ref[j] = s
            bm_ref[j] = jnp.max(s, axis=0, keepdims=True)

    def softmax_pv(ki, buf):
        src_ref, bm_ref = buf
        vb = vt_ref[ki]
        for j in range(2):
            for c0 in range(0, tq, ATT_QCOLS):
                cols = slice(c0, c0 + ATT_QCOLS)
                m_old = m_ref[j, :, cols]
                m_new = jnp.maximum(m_old, bm_ref[j, :, cols])
                alpha = jnp.exp2(m_old - m_new)
                p = jnp.exp2(src_ref[j, :, cols] - m_new)
                l_ref[j, :, cols] = alpha * l_ref[j, :, cols] + jnp.sum(p, axis=0, keepdims=True)
                acc_ref[j, :, cols] = alpha * acc_ref[j, :, cols] + jnp.dot(
                    vb, p.astype(BF16), preferred_element_type=F32)
                m_ref[j, :, cols] = m_new

    buf_a = (sa_ref, bma_ref)
    buf_b = (sb_ref, bmb_ref)

    @pl.when(qi == 0)
    def _():
        scores(0, buf_a, True)
        softmax_pv(0, buf_a)

    @pl.when(qi > 0)
    def _():
        scores(0, buf_a, False)

    def pair(t, carry):
        k0 = 2 * t
        scores(k0 + 1, buf_b, False)
        softmax_pv(k0, buf_a)
        scores(k0 + 2, buf_a, False)
        softmax_pv(k0 + 1, buf_b)
        return carry

    n_pairs = jnp.maximum(qi - 1, 0) // 2
    lax.fori_loop(0, n_pairs, pair, 0)
    k_rest = 2 * n_pairs

    @pl.when(qi % 2 == 1)
    def _():
        scores(qi, buf_b, True)
        softmax_pv(k_rest, buf_a)
        softmax_pv(qi, buf_b)

    @pl.when(jnp.logical_and(qi % 2 == 0, qi > 0))
    def _():
        scores(k_rest + 1, buf_b, False)
        softmax_pv(k_rest, buf_a)
        scores(qi, buf_a, True)
        softmax_pv(k_rest + 1, buf_b)
        softmax_pv(qi, buf_a)

    lam = (jnp.exp(jnp.sum(lq1_ref[...] * lk1_ref[...], axis=-1, keepdims=True))
           - jnp.exp(jnp.sum(lq2_ref[...] * lk2_ref[...], axis=-1, keepdims=True))
           + lam_init)
    o = acc_ref[0] / l_ref[0] - lam * (acc_ref[1] / l_ref[1])
    ms = jnp.mean(o * o, axis=0, keepdims=True)
    on = (o * lax.rsqrt(ms + EPS) * sg_ref[...]) * (1.0 - lam_init)
    o_ref[...] = on.T.astype(o_ref.dtype)


def _attention(qt, k, vt, lq1, lk1, lq2, lk2, subln_g, lam_init, e):
    b, s, _ = k.shape
    nk, tk = vt.shape[1], vt.shape[3]
    tq = qt.shape[3]
    assert tq == tk
    return pl.pallas_call(
        functools.partial(_attn_kernel, lam_init=lam_init),
        grid=(b, ATT_HEADS, s // tq),
        in_specs=[
            pl.BlockSpec((None, None, 2 * ATT_QK_DIM, tq), lambda bi, h, i: (bi, i, h, 0)),
            pl.BlockSpec((None, s, 2 * ATT_QK_DIM), lambda bi, h, i: (bi, 0, h)),
            pl.BlockSpec((None, nk, ATT_V_DIM, tk), lambda bi, h, i: (bi, 0, h, 0)),
            _layer_spec(lq1, e), _layer_spec(lk1, e), _layer_spec(lq2, e), _layer_spec(lk2, e),
            _layer_spec(subln_g, e),
        ],
        out_specs=pl.BlockSpec((None, tq, ATT_V_DIM), lambda bi, h, i: (bi, i, h)),
        out_shape=jax.ShapeDtypeStruct((b, s, ATT_WIDTH), BF16),
        scratch_shapes=[
            pltpu.VMEM((2, tk, tq), F32),
            pltpu.VMEM((2, tk, tq), F32),
            pltpu.VMEM((2, 1, tq), F32),
            pltpu.VMEM((2, 1, tq), F32),
            pltpu.VMEM((2, 1, tq), F32),
            pltpu.VMEM((2, 1, tq), F32),
            pltpu.VMEM((2, ATT_V_DIM, tq), F32),
        ],
        compiler_params=_params(3),
        name="diff_attn",
    )(qt, k, vt, lq1, lk1, lq2, lk2, subln_g)


def _conv_kernel(x_ref, mod_ref, ng_ref, win_ref, cw_ref, wout_ref, out_ref, proj_ref, vh_ref):
    t = pl.program_id(1)
    tm, d = x_ref.shape
    x = x_ref[...]
    shift = mod_ref[3:4, :]
    scale = mod_ref[4:5, :]
    gate = mod_ref[5:6, :]
    h = _modulated_norm(x, ng_ref[...], scale, shift).astype(BF16)
    proj_ref[...] = jnp.dot(h, win_ref[...], preferred_element_type=F32)

    @pl.when(t == 0)
    def _():
        vh_ref[0:CONV_HIST, :] = jnp.zeros((CONV_HIST, d), F32)

    vh_ref[CONV_HIST:CONV_HIST + tm, :] = proj_ref[:, d:2 * d] * proj_ref[:, 2 * d:3 * d]
    y = cw_ref[CONV_K - 1:CONV_K, :] * vh_ref[CONV_HIST:CONV_HIST + tm, :]
    for j in range(1, CONV_K):
        y = y + cw_ref[CONV_K - 1 - j:CONV_K - j, :] * vh_ref[CONV_HIST - j:CONV_HIST - j + tm, :]
    z = (proj_ref[:, 0:d] * y).astype(BF16)
    out_ref[...] = x + gate * jnp.dot(z, wout_ref[...], preferred_element_type=F32)
    vh_ref[0:CONV_HIST, :] = vh_ref[tm:tm + CONV_HIST, :]


def _conv_mixer(x, mod, norm_g, w_in, conv_w, w_out, layer, o_idx):
    b, s, d = x.shape
    tm = ROW_TILE
    row = pl.BlockSpec((None, tm, d), lambda bi, i: (bi, i, 0))
    return pl.pallas_call(
        _conv_kernel,
        grid=(b, s // tm),
        in_specs=[row, _mod_spec(mod, layer), _layer_spec(norm_g, layer, 1), _layer_spec(w_in, o_idx),
                  _layer_spec(conv_w, o_idx), _layer_spec(w_out, o_idx)],
        out_specs=row,
        out_shape=jax.ShapeDtypeStruct(x.shape, F32),
        scratch_shapes=[pltpu.VMEM((tm, 3 * d), F32), pltpu.VMEM((CONV_HIST + tm, d), F32)],
        compiler_params=_params(2),
        name="conv_mixer",
    )(x, mod, norm_g, w_in, conv_w, w_out)


def kernel(x, c, norm_g, w_ada, b_ada, ffn_wg, ffn_wu, ffn_wd, w_in_ab, qk_norm_g, lambda_q1, lambda_k1,
           lambda_q2, lambda_k2, subln_g, pool_w, pool_scale, w_out_ab, w_in_c, conv_w, w_out_c):
    depth = norm_g.shape[0]
    d = x.shape[-1]
    n_even = w_in_ab.shape[0]
    mod = _ada_modulation(c, w_ada, b_ada)

    wg = ffn_wg.astype(BF16)
    wu = ffn_wu.astype(BF16)
    wd = ffn_wd.astype(BF16)
    w_in_ab16 = w_in_ab.astype(BF16)
    w_out_ab16 = w_out_ab.astype(BF16)
    pool_w16 = pool_w.astype(BF16)
    w_in_c16 = w_in_c.astype(BF16)
    w_out_c16 = w_out_c.astype(BF16)
    grp = jnp.arange(V7X_MXU_DIM) // ATT_QK_DIM
    seg = (grp[:, None] == grp[None, :]).astype(BF16)
    qk_gain = jnp.tile(qk_norm_g, (1, 1, QK_COLS // ATT_QK_DIM))
    norm_g4 = norm_g.reshape(depth, 3, 1, d)
    pool_scale3 = pool_scale.reshape(n_even, 1, POOL_WIDTH)
    lam_rows = [a.reshape(n_even, 1, ATT_QK_DIM) for a in (lambda_q1, lambda_k1, lambda_q2, lambda_k2)]
    subln_col = subln_g.reshape(n_even, ATT_V_DIM, 1)

    for l in range(depth):
        x = _ffn(x, mod, norm_g4, wg, wu, wd, l, 0)
        if l % 2 == 0:
            e = l // 2
            lam_init = 0.8 - 0.6 * math.exp(-0.3 * l)
            qt, k, vt, p = _inproj(x, mod, norm_g4, w_in_ab16, qk_gain, seg, pool_w16, pool_scale3, l, e)
            o = _attention(qt, k, vt, *lam_rows, subln_col, lam_init, e)
            x = _ffn(x, mod, norm_g4, wg, wu, wd, l, 1, mixer=(o, p, w_out_ab16, e))
        else:
            x = _conv_mixer(x, mod, norm_g4, w_in_c16, conv_w, w_out_c16, l, l // 2)
            x = _ffn(x, mod, norm_g4, wg, wu, wd, l, 1)
    return x
```

```python
import functools
import math

import jax
import jax.numpy as jnp
from jax import lax
from jax.experimental import pallas as pl
from jax.experimental.pallas import tpu as pltpu

F32 = jnp.float32
BF16 = jnp.bfloat16

EPS = 1e-6
N_MOD = 9
ATT_HEADS = 4
ATT_QK_DIM = 64
ATT_V_DIM = 128
QK_COLS = 512
ATT_WIDTH = 512
POOL_WINDOWS = (2, 4, 8, 16)
POOL_WIDTH = 512
POOL_GROUP_DIM = 128
CONV_K = 3

V7X_MXU_DIM = 256
VMEM_LIMIT_BYTES = 56 * 1024 * 1024

ROW_TILE = 512
FFN_TILE = 1024
CONV_TILE = 1024
ATT_QCOLS = V7X_MXU_DIM
FF_CHUNK = V7X_MXU_DIM
CAST_STEPS_PER_SET = 16
POOL_HIST = 16
CONV_HIST = 8
LOG2E = math.log2(math.e)


def _params(n_axes):
    return pltpu.CompilerParams(
        dimension_semantics=("arbitrary",) * n_axes, vmem_limit_bytes=VMEM_LIMIT_BYTES)


def _layer_spec(arr, *lead):
    tail = arr.shape[len(lead):]
    block = (None,) * len(lead) + tail
    index = tuple(lead) + (0,) * len(tail)
    return pl.BlockSpec(block, lambda *_: index, pipeline_mode=pl.Buffered(1))


def _modulated_norm(x, g, scale, shift):
    ms = jnp.mean(x * x, axis=-1, keepdims=True)
    y = x * lax.rsqrt(ms + EPS)
    return (y * g) * (1.0 + scale) + shift


def _silu(x):
    return x * jax.nn.sigmoid(x)


def _ada_kernel(c_ref, w_ref, b_ref, o_ref):
    c_act = _silu(c_ref[...]).astype(BF16)
    w = w_ref[...].astype(BF16)
    o_ref[...] = jnp.dot(c_act, w, preferred_element_type=F32) + b_ref[...]


def _ada_modulation(c, w_ada, b_ada):
    depth, d, _ = w_ada.shape
    b = c.shape[0]
    rows = 8
    c_pad = jnp.zeros((rows, d), F32).at[:b].set(c)
    out = pl.pallas_call(
        _ada_kernel,
        grid=(depth, N_MOD),
        in_specs=[
            pl.BlockSpec((rows, d), lambda l, j: (0, 0)),
            pl.BlockSpec((None, d, d), lambda l, j: (l, 0, j)),
            pl.BlockSpec((None, None, 1, d), lambda l, j: (l, j, 0, 0)),
        ],
        out_specs=pl.BlockSpec((None, rows, d), lambda l, j: (l, 0, j)),
        out_shape=jax.ShapeDtypeStruct((depth, rows, N_MOD * d), F32),
        compiler_params=_params(2),
        name="ada_mod",
    )(c_pad, w_ada, b_ada.reshape(depth, N_MOD, 1, d))
    return out[:, :b].reshape(depth, b, N_MOD, d)


def _mod_spec(mod, layer):
    d = mod.shape[-1]
    return pl.BlockSpec((None, None, N_MOD, d), lambda bi, i: (layer, bi, 0, 0))


def _ffn_kernel(x_ref, mod_ref, ng_ref, wg_ref, wu_ref, wd_ref, *rest, sub, mix):
    if mix:
        ao_ref, po_ref, wo_ref, o_ref, a_ref = rest
        cat = jnp.concatenate([ao_ref[...], po_ref[...]], axis=-1)
        x = x_ref[...] + mod_ref[5:6, :] * jnp.dot(cat, wo_ref[...], preferred_element_type=F32)
    else:
        o_ref, a_ref = rest
        x = x_ref[...]
    shift = mod_ref[3 * sub:3 * sub + 1, :]
    scale = mod_ref[3 * sub + 1:3 * sub + 2, :]
    gate = mod_ref[3 * sub + 2:3 * sub + 3, :]
    h = _modulated_norm(x, ng_ref[...], scale, shift).astype(BF16)
    d_ff = a_ref.shape[1]
    for c0 in range(0, d_ff, FF_CHUNK):
        c1 = min(c0 + FF_CHUNK, d_ff)
        g = jnp.dot(h, wg_ref[:, c0:c1], preferred_element_type=F32)
        u = jnp.dot(h, wu_ref[:, c0:c1], preferred_element_type=F32)
        a_ref[:, c0:c1] = (_silu(g) * u).astype(BF16)
    y = jnp.dot(a_ref[...], wd_ref[...], preferred_element_type=F32)
    o_ref[...] = x + (0.5 * gate) * y


def _row_set_spec(w_rows, idx, set_rows):
    return pl.BlockSpec((set_rows, w_rows.shape[1]), lambda *_: (idx, 0), pipeline_mode=pl.Buffered(1))


def _ffn(x, mod, norm_g, weights, layer, half, mixer=None):
    b, s, d = x.shape
    (wg, wu, wd), w_idx = weights
    d_ff = wg.shape[1]
    tm = FFN_TILE
    sub = 2 * half
    row_spec = pl.BlockSpec((None, tm, d), lambda bi, i: (bi, i, 0))
    in_specs = [
        row_spec,
        _mod_spec(mod, layer),
        _layer_spec(norm_g, layer, sub),
        _row_set_spec(wg, w_idx, d),
        _row_set_spec(wu, w_idx, d),
        _row_set_spec(wd, w_idx, d_ff),
    ]
    args = [x, mod, norm_g, wg, wu, wd]
    if mixer is not None:
        ao, po, w_out, e = mixer
        half_spec = pl.BlockSpec((None, tm, ATT_WIDTH), lambda bi, i: (bi, i, 0))
        in_specs += [half_spec, half_spec, _layer_spec(w_out, e)]
        args += [ao, po, w_out]
    return pl.pallas_call(
        functools.partial(_ffn_kernel, sub=sub, mix=mixer is not None),
        grid=(b, s // tm),
        in_specs=in_specs,
        out_specs=row_spec,
        out_shape=jax.ShapeDtypeStruct(x.shape, F32),
        scratch_shapes=[pltpu.VMEM((tm, d_ff), BF16)],
        compiler_params=_params(2),
        name=f"ffn{sub}" + ("_mix" if mixer is not None else ""),
    )(*args)


def _group_mean_sq(xc, seg):
    sq = xc * xc
    hi = sq.astype(BF16)
    lo = (sq - hi.astype(F32)).astype(BF16)
    tot = jnp.dot(hi, seg, preferred_element_type=F32) + jnp.dot(lo, seg, preferred_element_type=F32)
    return tot * (1.0 / ATT_QK_DIM)


def _inproj_kernel(x_ref, mod_ref, ng_ref, win_ref, qkg_ref, seg_ref, pw_ref, ps_ref,
                   qt_ref, k_ref, vt_ref, p_ref, proj_ref, uh_ref):
    t = pl.program_id(1)
    tm = x_ref.shape[0]
    x = x_ref[...]
    shift = mod_ref[3:4, :]
    scale = mod_ref[4:5, :]
    h = _modulated_norm(x, ng_ref[...], scale, shift).astype(BF16)
    proj_ref[...] = jnp.dot(h, win_ref[...], preferred_element_type=F32)

    seg = seg_ref[...]
    q_scale = ATT_QK_DIM ** -0.5 * LOG2E
    for which in range(2):
        gain = qkg_ref[which:which + 1, :]
        if which == 0:
            gain = gain * q_scale
        for c0 in range(0, QK_COLS, V7X_MXU_DIM):
            xc = proj_ref[:, which * QK_COLS + c0:which * QK_COLS + c0 + V7X_MXU_DIM]
            ms = _group_mean_sq(xc, seg)
            normed = xc * lax.rsqrt(ms + EPS) * gain[:, c0:c0 + V7X_MXU_DIM]
            if which == 0:
                qt_ref[c0:c0 + V7X_MXU_DIM, :] = normed.T.astype(BF16)
            else:
                k_ref[:, c0:c0 + V7X_MXU_DIM] = normed.astype(BF16)

    v = proj_ref[:, 2 * QK_COLS:2 * QK_COLS + ATT_WIDTH]
    vt_ref[...] = v.T.astype(BF16)

    u0 = 2 * QK_COLS + ATT_WIDTH

    @pl.when(t == 0)
    def _():
        uh_ref[0:POOL_HIST, :] = jnp.zeros((POOL_HIST, POOL_WIDTH), F32)

    uh_ref[POOL_HIST:POOL_HIST + tm, :] = proj_ref[:, u0:u0 + POOL_WIDTH]
    pos = t * tm + lax.broadcasted_iota(jnp.int32, (tm, 1), 0)
    for g, w in enumerate(POOL_WINDOWS):
        cols = slice(g * POOL_GROUP_DIM, (g + 1) * POOL_GROUP_DIM)
        cur = uh_ref[POOL_HIST:POOL_HIST + tm, cols]
        acc = cur
        for j in range(1, w):
            acc = acc + uh_ref[POOL_HIST - j:POOL_HIST - j + tm, cols]
        cnt = jnp.minimum(pos + 1, w).astype(F32)
        dlt = (acc / cnt - cur).astype(BF16)
        y = jnp.dot(dlt, pw_ref[g], preferred_element_type=F32)
        p_ref[:, cols] = (y * ps_ref[:, cols]).astype(BF16)
    uh_ref[0:POOL_HIST, :] = uh_ref[tm:tm + POOL_HIST, :]


def _inproj(x, mod, norm_g, w_in, qk_gain, seg, pool_w, pool_scale, layer, e):
    b, s, d = x.shape
    tm = ROW_TILE
    width = w_in.shape[-1]
    row_in = pl.BlockSpec((None, tm, d), lambda bi, i: (bi, i, 0))
    row_out = pl.BlockSpec((None, tm, QK_COLS), lambda bi, i: (bi, i, 0))
    return pl.pallas_call(
        _inproj_kernel,
        grid=(b, s // tm),
        in_specs=[
            row_in,
            _mod_spec(mod, layer),
            _layer_spec(norm_g, layer, 1),
            _layer_spec(w_in, e),
            _layer_spec(qk_gain, e),
            _layer_spec(seg),
            _layer_spec(pool_w, e),
            _layer_spec(pool_scale, e),
        ],
        out_specs=[
            pl.BlockSpec((None, None, QK_COLS, tm), lambda bi, i: (bi, i, 0, 0)),
            row_out,
            pl.BlockSpec((None, None, ATT_WIDTH, tm), lambda bi, i: (bi, i, 0, 0)),
            row_out,
        ],
        out_shape=[
            jax.ShapeDtypeStruct((b, s // tm, QK_COLS, tm), BF16),
            jax.ShapeDtypeStruct((b, s, QK_COLS), BF16),
            jax.ShapeDtypeStruct((b, s // tm, ATT_WIDTH, tm), BF16),
            jax.ShapeDtypeStruct((b, s, POOL_WIDTH), BF16),
        ],
        scratch_shapes=[pltpu.VMEM((tm, width), F32), pltpu.VMEM((POOL_HIST + tm, POOL_WIDTH), F32)],
        compiler_params=_params(2),
        name="inproj_ab",
    )(x, mod, norm_g, w_in, qk_gain, seg, pool_w, pool_scale)


def _attn_kernel(qt_ref, k_ref, vt_ref, lq1_ref, lk1_ref, lq2_ref, lk2_ref, sg_ref,
                 wg_f32_ref, wu_f32_ref, wd_f32_ref, o_ref, wg16_ref, wu16_ref, wd16_ref,
                 sa_ref, sb_ref, bma_ref, bmb_ref, m_ref, l_ref, acc_ref, *, lam_init):
    wg16_ref[...] = wg_f32_ref[...].astype(BF16)
    wu16_ref[...] = wu_f32_ref[...].astype(BF16)
    wd16_ref[...] = wd_f32_ref[...].astype(BF16)

    qi = pl.program_id(2)
    tq = qt_ref.shape[1]
    tk = vt_ref.shape[2]
    qt = qt_ref[...]
    feat = lax.broadcasted_iota(jnp.int32, qt.shape, 0)
    zero = jnp.zeros_like(qt)
    q_maps = (jnp.where(feat < ATT_QK_DIM, qt, zero), jnp.where(feat >= ATT_QK_DIM, qt, zero))

    m_ref[...] = jnp.full(m_ref.shape, -jnp.inf, F32)
    l_ref[...] = jnp.zeros(l_ref.shape, F32)
    acc_ref[...] = jnp.zeros(acc_ref.shape, F32)

    def scores(ki, buf, diagonal):
        dst_ref, bm_ref = buf
        kb = k_ref[pl.ds(pl.multiple_of(ki * tk, tk), tk), :]
        for j in range(2):
            s = jnp.dot(kb, q_maps[j], preferred_element_type=F32)
            if diagonal:
                kpos = lax.broadcasted_iota(jnp.int32, s.shape, 0)
                qpos = lax.broadcasted_iota(jnp.int32, s.shape, 1)
                s = jnp.where(kpos <= qpos, s, -jnp.inf)
            dst_ref[j] = s
            bm_ref[j] = jnp.max(s, axis=0, keepdims=True)

    def softmax_pv(ki, buf):
        src_ref, bm_ref = buf
        vb = vt_ref[ki]
        for j in range(2):
            for c0 in range(0, tq, ATT_QCOLS):
                cols = slice(c0, c0 + ATT_QCOLS)
                m_old = m_ref[j, :, cols]
                m_new = jnp.maximum(m_old, bm_ref[j, :, cols])
                alpha = jnp.exp2(m_old - m_new)
                p = jnp.exp2(src_ref[j, :, cols] - m_new)
                l_ref[j, :, cols] = alpha * l_ref[j, :, cols] + jnp.sum(p, axis=0, keepdims=True)
                acc_ref[j, :, cols] = alpha * acc_ref[j, :, cols] + jnp.dot(
                    vb, p.astype(BF16), preferred_element_type=F32)
                m_ref[j, :, cols] = m_new

    buf_a = (sa_ref, bma_ref)
    buf_b = (sb_ref, bmb_ref)

    @pl.when(qi == 0)
    def _():
        scores(0, buf_a, True)
        softmax_pv(0, buf_a)

    @pl.when(qi > 0)
    def _():
        scores(0, buf_a, False)

    def pair(t, carry):
        k0 = 2 * t
        scores(k0 + 1, buf_b, False)
        softmax_pv(k0, buf_a)
        scores(k0 + 2, buf_a, False)
        softmax_pv(k0 + 1, buf_b)
        return carry

    n_pairs = jnp.maximum(qi - 1, 0) // 2
    lax.fori_loop(0, n_pairs, pair, 0)
    k_rest = 2 * n_pairs

    @pl.when(qi % 2 == 1)
    def _():
        scores(qi, buf_b, True)
        softmax_pv(k_rest, buf_a)
        softmax_pv(qi, buf_b)

    @pl.when(jnp.logical_and(qi % 2 == 0, qi > 0))
    def _():
        scores(k_rest + 1, buf_b, False)
        softmax_pv(k_rest, buf_a)
        scores(qi, buf_a, True)
        softmax_pv(k_rest + 1, buf_b)
        softmax_pv(qi, buf_a)

    lam = (jnp.exp(jnp.sum(lq1_ref[...] * lk1_ref[...], axis=-1, keepdims=True))
           - jnp.exp(jnp.sum(lq2_ref[...] * lk2_ref[...], axis=-1, keepdims=True))
           + lam_init)
    o = acc_ref[0] / l_ref[0] - lam * (acc_ref[1] / l_ref[1])
    ms = jnp.mean(o * o, axis=0, keepdims=True)
    on = (o * lax.rsqrt(ms + EPS) * sg_ref[...]) * (1.0 - lam_init)
    o_ref[...] = on.T.astype(o_ref.dtype)


def _attention(qt, k, vt, lq1, lk1, lq2, lk2, subln_g, lam_init, e, ffn_w_rows, n_total_sets, first_set,
               n_sets):
    b, s, _ = k.shape
    nk, tk = vt.shape[1], vt.shape[3]
    tq = qt.shape[3]
    nq = s // tq
    assert tq == tk
    n_cast = n_sets * CAST_STEPS_PER_SET
    assert n_cast <= b * ATT_HEADS * nq

    def cast_specs(w_rows):
        set_rows = w_rows.shape[0] // n_total_sets
        rows = set_rows // CAST_STEPS_PER_SET
        cols = w_rows.shape[1]
        first_blk = first_set * CAST_STEPS_PER_SET

        def chunk(bi, h, i):
            return jnp.minimum((bi * ATT_HEADS + h) * nq + i, n_cast - 1)

        in_spec = pl.BlockSpec((rows, cols), lambda bi, h, i: (first_blk + chunk(bi, h, i), 0))
        out_spec = pl.BlockSpec((rows, cols), lambda bi, h, i: (chunk(bi, h, i), 0))
        return in_spec, out_spec, jax.ShapeDtypeStruct((n_sets * set_rows, cols), BF16)

    cast = [cast_specs(w) for w in ffn_w_rows]
    return pl.pallas_call(
        functools.partial(_attn_kernel, lam_init=lam_init),
        grid=(b, ATT_HEADS, nq),
        in_specs=[
            pl.BlockSpec((None, None, 2 * ATT_QK_DIM, tq), lambda bi, h, i: (bi, i, h, 0)),
            pl.BlockSpec((None, s, 2 * ATT_QK_DIM), lambda bi, h, i: (bi, 0, h)),
            pl.BlockSpec((None, nk, ATT_V_DIM, tk), lambda bi, h, i: (bi, 0, h, 0)),
            _layer_spec(lq1, e), _layer_spec(lk1, e), _layer_spec(lq2, e), _layer_spec(lk2, e),
            _layer_spec(subln_g, e),
        ] + [c[0] for c in cast],
        out_specs=[pl.BlockSpec((None, tq, ATT_V_DIM), lambda bi, h, i: (bi, i, h))] + [c[1] for c in cast],
        out_shape=[jax.ShapeDtypeStruct((b, s, ATT_WIDTH), BF16)] + [c[2] for c in cast],
        scratch_shapes=[
            pltpu.VMEM((2, tk, tq), F32),
            pltpu.VMEM((2, tk, tq), F32),
            pltpu.VMEM((2, 1, tq), F32),
            pltpu.VMEM((2, 1, tq), F32),
            pltpu.VMEM((2, 1, tq), F32),
            pltpu.VMEM((2, 1, tq), F32),
            pltpu.VMEM((2, ATT_V_DIM, tq), F32),
        ],
        compiler_params=_params(3),
        name="diff_attn",
    )(qt, k, vt, lq1, lk1, lq2, lk2, subln_g, *ffn_w_rows)


def _conv_kernel(x_ref, mod_ref, ng_ref, win_ref, cw_ref, wout_ref, out_ref, proj_ref, vh_ref):
    t = pl.program_id(1)
    tm, d = x_ref.shape
    x = x_ref[...]
    shift = mod_ref[3:4, :]
    scale = mod_ref[4:5, :]
    gate = mod_ref[5:6, :]
    h = _modulated_norm(x, ng_ref[...], scale, shift).astype(BF16)
    proj_ref[...] = jnp.dot(h, win_ref[...], preferred_element_type=F32)

    @pl.when(t == 0)
    def _():
        vh_ref[0:CONV_HIST, :] = jnp.zeros((CONV_HIST, d), F32)

    vh_ref[CONV_HIST:CONV_HIST + tm, :] = proj_ref[:, d:2 * d] * proj_ref[:, 2 * d:3 * d]
    y = cw_ref[CONV_K - 1:CONV_K, :] * vh_ref[CONV_HIST:CONV_HIST + tm, :]
    for j in range(1, CONV_K):
        y = y + cw_ref[CONV_K - 1 - j:CONV_K - j, :] * vh_ref[CONV_HIST - j:CONV_HIST - j + tm, :]
    z = (proj_ref[:, 0:d] * y).astype(BF16)
    out_ref[...] = x + gate * jnp.dot(z, wout_ref[...], preferred_element_type=F32)
    vh_ref[0:CONV_HIST, :] = vh_ref[tm:tm + CONV_HIST, :]


def _conv_mixer(x, mod, norm_g, w_in, conv_w, w_out, layer, o_idx):
    b, s, d = x.shape
    tm = CONV_TILE
    row = pl.BlockSpec((None, tm, d), lambda bi, i: (bi, i, 0))
    return pl.pallas_call(
        _conv_kernel,
        grid=(b, s // tm),
        in_specs=[row, _mod_spec(mod, layer), _layer_spec(norm_g, layer, 1), _layer_spec(w_in, o_idx),
                  _layer_spec(conv_w, o_idx), _layer_spec(w_out, o_idx)],
        out_specs=row,
        out_shape=jax.ShapeDtypeStruct(x.shape, F32),
        scratch_shapes=[pltpu.VMEM((tm, 3 * d), F32), pltpu.VMEM((CONV_HIST + tm, d), F32)],
        compiler_params=_params(2),
        name="conv_mixer",
    )(x, mod, norm_g, w_in, conv_w, w_out)


def kernel(x, c, norm_g, w_ada, b_ada, ffn_wg, ffn_wu, ffn_wd, w_in_ab, qk_norm_g, lambda_q1, lambda_k1,
           lambda_q2, lambda_k2, subln_g, pool_w, pool_scale, w_out_ab, w_in_c, conv_w, w_out_c):
    depth = norm_g.shape[0]
    d = x.shape[-1]
    n_even = w_in_ab.shape[0]
    mod = _ada_modulation(c, w_ada, b_ada)

    n_sets_total = 2 * depth
    d_ff = ffn_wg.shape[-1]
    ffn_w_rows = (ffn_wg.reshape(n_sets_total * d, d_ff), ffn_wu.reshape(n_sets_total * d, d_ff),
                  ffn_wd.reshape(n_sets_total * d_ff, d))
    ffn_sets = {0: ((ffn_wg[0, 0].astype(BF16), ffn_wu[0, 0].astype(BF16), ffn_wd[0, 0].astype(BF16)), 0)}
    w_in_ab16 = w_in_ab.astype(BF16)
    w_out_ab16 = w_out_ab.astype(BF16)
    pool_w16 = pool_w.astype(BF16)
    w_in_c16 = w_in_c.astype(BF16)
    w_out_c16 = w_out_c.astype(BF16)
    grp = jnp.arange(V7X_MXU_DIM) // ATT_QK_DIM
    seg = (grp[:, None] == grp[None, :]).astype(BF16)
    qk_gain = jnp.tile(qk_norm_g, (1, 1, QK_COLS // ATT_QK_DIM))
    norm_g4 = norm_g.reshape(depth, 3, 1, d)
    pool_scale3 = pool_scale.reshape(n_even, 1, POOL_WIDTH)
    lam_rows = [a.reshape(n_even, 1, ATT_QK_DIM) for a in (lambda_q1, lambda_k1, lambda_q2, lambda_k2)]
    subln_col = subln_g.reshape(n_even, ATT_V_DIM, 1)

    for l in range(depth):
        x = _ffn(x, mod, norm_g4, ffn_sets[2 * l], l, 0)
        if l % 2 == 0:
            e = l // 2
            lam_init = 0.8 - 0.6 * math.exp(-0.3 * l)
            qt, k, vt, p = _inproj(x, mod, norm_g4, w_in_ab16, qk_gain, seg, pool_w16, pool_scale3, l, e)
            first_set = 2 * l + 1
            n_sets = min(4, n_sets_total - first_set)
            o, *cast_w = _attention(qt, k, vt, *lam_rows, subln_col, lam_init, e, ffn_w_rows, n_sets_total,
                                    first_set, n_sets)
            for j in range(n_sets):
                ffn_sets[first_set + j] = (tuple(cast_w), j)
            x = _ffn(x, mod, norm_g4, ffn_sets[2 * l + 1], l, 1, mixer=(o, p, w_out_ab16, e))
        else:
            x = _conv_mixer(x, mod, norm_g4, w_in_c16, conv_w, w_out_c16, l, l // 2)
            x = _ffn(x, mod, norm_g4, ffn_sets[2 * l + 1], l, 1)
    return x
```

```python
import functools
import math

import jax
import jax.numpy as jnp
from jax import lax
from jax.experimental import pallas as pl
from jax.experimental.pallas import tpu as pltpu

F32 = jnp.float32
BF16 = jnp.bfloat16

EPS = 1e-6
N_MOD = 9
ATT_HEADS = 4
ATT_QK_DIM = 64
ATT_V_DIM = 128
QK_COLS = 512
ATT_WIDTH = 512
POOL_WINDOWS = (2, 4, 8, 16)
POOL_WIDTH = 512
POOL_GROUP_DIM = 128
CONV_K = 3

V7X_MXU_DIM = 256
VMEM_LIMIT_BYTES = 56 * 1024 * 1024

ROW_TILE = 512
FFN_TILE = 1024
CONV_TILE = 1024
ATT_QCOLS = V7X_MXU_DIM
FF_CHUNK = V7X_MXU_DIM
CAST_STEPS_PER_SET = 8
POOL_HIST = 16
CONV_HIST = 8
LOG2E = math.log2(math.e)


def _params(n_axes):
    return pltpu.CompilerParams(
        dimension_semantics=("arbitrary",) * n_axes, vmem_limit_bytes=VMEM_LIMIT_BYTES)


def _layer_spec(arr, *lead):
    tail = arr.shape[len(lead):]
    block = (None,) * len(lead) + tail
    index = tuple(lead) + (0,) * len(tail)
    return pl.BlockSpec(block, lambda *_: index, pipeline_mode=pl.Buffered(1))


def _modulated_norm(x, g, scale, shift):
    ms = jnp.mean(x * x, axis=-1, keepdims=True)
    y = x * lax.rsqrt(ms + EPS)
    return (y * g) * (1.0 + scale) + shift


def _silu(x):
    return x * jax.nn.sigmoid(x)


def _ada_kernel(c_ref, w_ref, b_ref, o_ref):
    c_act = _silu(c_ref[...]).astype(BF16)
    w = w_ref[...].astype(BF16)
    o_ref[...] = jnp.dot(c_act, w, preferred_element_type=F32) + b_ref[...]


def _ada_modulation(c, w_ada, b_ada):
    depth, d, _ = w_ada.shape
    b = c.shape[0]
    rows = 8
    c_pad = jnp.zeros((rows, d), F32).at[:b].set(c)
    out = pl.pallas_call(
        _ada_kernel,
        grid=(depth, N_MOD),
        in_specs=[
            pl.BlockSpec((rows, d), lambda l, j: (0, 0)),
            pl.BlockSpec((None, d, d), lambda l, j: (l, 0, j)),
            pl.BlockSpec((None, None, 1, d), lambda l, j: (l, j, 0, 0)),
        ],
        out_specs=pl.BlockSpec((None, rows, d), lambda l, j: (l, 0, j)),
        out_shape=jax.ShapeDtypeStruct((depth, rows, N_MOD * d), F32),
        compiler_params=_params(2),
        name="ada_mod",
    )(c_pad, w_ada, b_ada.reshape(depth, N_MOD, 1, d))
    return out[:, :b].reshape(depth, b, N_MOD, d)


def _mod_spec(mod, layer):
    d = mod.shape[-1]
    return pl.BlockSpec((None, None, N_MOD, d), lambda bi, i: (layer, bi, 0, 0))


def _ffn_kernel(x_ref, mod_ref, ng_ref, wg_ref, wu_ref, wd_ref, *rest, sub, mix):
    if mix:
        ao_ref, po_ref, wo_ref, o_ref, a_ref = rest
        cat = jnp.concatenate([ao_ref[...], po_ref[...]], axis=-1)
        x = x_ref[...] + mod_ref[5:6, :] * jnp.dot(cat, wo_ref[...], preferred_element_type=F32)
    else:
        o_ref, a_ref = rest
        x = x_ref[...]
    shift = mod_ref[3 * sub:3 * sub + 1, :]
    scale = mod_ref[3 * sub + 1:3 * sub + 2, :]
    gate = mod_ref[3 * sub + 2:3 * sub + 3, :]
    h = _modulated_norm(x, ng_ref[...], scale, shift).astype(BF16)
    d_ff = a_ref.shape[1]
    for c0 in range(0, d_ff, FF_CHUNK):
        c1 = min(c0 + FF_CHUNK, d_ff)
        g = jnp.dot(h, wg_ref[:, c0:c1], preferred_element_type=F32)
        u = jnp.dot(h, wu_ref[:, c0:c1], preferred_element_type=F32)
        a_ref[:, c0:c1] = (_silu(g) * u).astype(BF16)
    y = jnp.dot(a_ref[...], wd_ref[...], preferred_element_type=F32)
    o_ref[...] = x + (0.5 * gate) * y


def _row_set_spec(w_rows, idx, set_rows):
    return pl.BlockSpec((set_rows, w_rows.shape[1]), lambda *_: (idx, 0), pipeline_mode=pl.Buffered(1))


def _ffn(x, mod, norm_g, weights, layer, half, mixer=None):
    b, s, d = x.shape
    (wg, wu, wd), w_idx = weights
    d_ff = wg.shape[1]
    tm = FFN_TILE
    sub = 2 * half
    row_spec = pl.BlockSpec((None, tm, d), lambda bi, i: (bi, i, 0))
    in_specs = [
        row_spec,
        _mod_spec(mod, layer),
        _layer_spec(norm_g, layer, sub),
        _row_set_spec(wg, w_idx, d),
        _row_set_spec(wu, w_idx, d),
        _row_set_spec(wd, w_idx, d_ff),
    ]
    args = [x, mod, norm_g, wg, wu, wd]
    if mixer is not None:
        ao, po, (w_out, w_out_idx) = mixer
        half_spec = pl.BlockSpec((None, tm, ATT_WIDTH), lambda bi, i: (bi, i, 0))
        in_specs += [half_spec, half_spec, _row_set_spec(w_out, w_out_idx, d)]
        args += [ao, po, w_out]
    return pl.pallas_call(
        functools.partial(_ffn_kernel, sub=sub, mix=mixer is not None),
        grid=(b, s // tm),
        in_specs=in_specs,
        out_specs=row_spec,
        out_shape=jax.ShapeDtypeStruct(x.shape, F32),
        scratch_shapes=[pltpu.VMEM((tm, d_ff), BF16)],
        compiler_params=_params(2),
        name=f"ffn{sub}" + ("_mix" if mixer is not None else ""),
    )(*args)


def _group_mean_sq(xc, seg):
    sq = xc * xc
    hi = sq.astype(BF16)
    lo = (sq - hi.astype(F32)).astype(BF16)
    tot = jnp.dot(hi, seg, preferred_element_type=F32) + jnp.dot(lo, seg, preferred_element_type=F32)
    return tot * (1.0 / ATT_QK_DIM)


def _inproj_kernel(x_ref, mod_ref, ng_ref, win_ref, qkg_ref, seg_ref, pw_ref, ps_ref,
                   qt_ref, k_ref, vt_ref, p_ref, proj_ref, uh_ref):
    t = pl.program_id(1)
    tm = x_ref.shape[0]
    x = x_ref[...]
    shift = mod_ref[3:4, :]
    scale = mod_ref[4:5, :]
    h = _modulated_norm(x, ng_ref[...], scale, shift).astype(BF16)
    proj_ref[...] = jnp.dot(h, win_ref[...], preferred_element_type=F32)

    seg = seg_ref[...]
    q_scale = ATT_QK_DIM ** -0.5 * LOG2E
    for which in range(2):
        gain = qkg_ref[which:which + 1, :]
        if which == 0:
            gain = gain * q_scale
        for c0 in range(0, QK_COLS, V7X_MXU_DIM):
            xc = proj_ref[:, which * QK_COLS + c0:which * QK_COLS + c0 + V7X_MXU_DIM]
            ms = _group_mean_sq(xc, seg)
            normed = xc * lax.rsqrt(ms + EPS) * gain[:, c0:c0 + V7X_MXU_DIM]
            if which == 0:
                qt_ref[c0:c0 + V7X_MXU_DIM, :] = normed.T.astype(BF16)
            else:
                k_ref[:, c0:c0 + V7X_MXU_DIM] = normed.astype(BF16)

    v = proj_ref[:, 2 * QK_COLS:2 * QK_COLS + ATT_WIDTH]
    vt_ref[...] = v.T.astype(BF16)

    u0 = 2 * QK_COLS + ATT_WIDTH

    @pl.when(t == 0)
    def _():
        uh_ref[0:POOL_HIST, :] = jnp.zeros((POOL_HIST, POOL_WIDTH), F32)

    uh_ref[POOL_HIST:POOL_HIST + tm, :] = proj_ref[:, u0:u0 + POOL_WIDTH]
    pos = t * tm + lax.broadcasted_iota(jnp.int32, (tm, 1), 0)
    for g, w in enumerate(POOL_WINDOWS):
        cols = slice(g * POOL_GROUP_DIM, (g + 1) * POOL_GROUP_DIM)
        cur = uh_ref[POOL_HIST:POOL_HIST + tm, cols]
        acc = cur
        for j in range(1, w):
            acc = acc + uh_ref[POOL_HIST - j:POOL_HIST - j + tm, cols]
        cnt = jnp.minimum(pos + 1, w).astype(F32)
        dlt = (acc / cnt - cur).astype(BF16)
        y = jnp.dot(dlt, pw_ref[g], preferred_element_type=F32)
        p_ref[:, cols] = (y * ps_ref[:, cols]).astype(BF16)
    uh_ref[0:POOL_HIST, :] = uh_ref[tm:tm + POOL_HIST, :]


def _inproj(x, mod, norm_g, w_in_set, qk_gain, seg, pool_w, pool_scale, layer, e):
    b, s, d = x.shape
    tm = ROW_TILE
    w_in, w_in_idx = w_in_set
    width = w_in.shape[-1]
    row_in = pl.BlockSpec((None, tm, d), lambda bi, i: (bi, i, 0))
    row_out = pl.BlockSpec((None, tm, QK_COLS), lambda bi, i: (bi, i, 0))
    return pl.pallas_call(
        _inproj_kernel,
        grid=(b, s // tm),
        in_specs=[
            row_in,
            _mod_spec(mod, layer),
            _layer_spec(norm_g, layer, 1),
            _row_set_spec(w_in, w_in_idx, d),
            _layer_spec(qk_gain, e),
            _layer_spec(seg),
            _layer_spec(pool_w, e),
            _layer_spec(pool_scale, e),
        ],
        out_specs=[
            pl.BlockSpec((None, None, QK_COLS, tm), lambda bi, i: (bi, i, 0, 0)),
            row_out,
            pl.BlockSpec((None, None, ATT_WIDTH, tm), lambda bi, i: (bi, i, 0, 0)),
            row_out,
        ],
        out_shape=[
            jax.ShapeDtypeStruct((b, s // tm, QK_COLS, tm), BF16),
            jax.ShapeDtypeStruct((b, s, QK_COLS), BF16),
            jax.ShapeDtypeStruct((b, s // tm, ATT_WIDTH, tm), BF16),
            jax.ShapeDtypeStruct((b, s, POOL_WIDTH), BF16),
        ],
        scratch_shapes=[pltpu.VMEM((tm, width), F32), pltpu.VMEM((POOL_HIST + tm, POOL_WIDTH), F32)],
        compiler_params=_params(2),
        name="inproj_ab",
    )(x, mod, norm_g, w_in, qk_gain, seg, pool_w, pool_scale)


def _attn_kernel(qt_ref, k_ref, vt_ref, lq1_ref, lk1_ref, lq2_ref, lk2_ref, sg_ref, *rest, lam_init, n_cast):
    cast_src = rest[:n_cast]
    o_ref = rest[n_cast]
    cast_dst = rest[n_cast + 1:2 * n_cast + 1]
    (sa0_ref, sb0_ref, sa1_ref, sb1_ref, bma0_ref, bmb0_ref, bma1_ref, bmb1_ref,
     m_ref, l_ref, acc_ref) = rest[2 * n_cast + 1:]
    for src_ref, dst_ref in zip(cast_src, cast_dst):
        dst_ref[...] = src_ref[...].astype(BF16)

    qi = pl.program_id(1)
    tq = qt_ref.shape[1]
    tk = vt_ref.shape[2]
    width = 2 * ATT_QK_DIM
    first_block_shift = jnp.where(qi == 0, 0, tk)
    lam = (jnp.exp(jnp.sum(lq1_ref[...] * lk1_ref[...], axis=-1, keepdims=True))
           - jnp.exp(jnp.sum(lq2_ref[...] * lk2_ref[...], axis=-1, keepdims=True))
           + lam_init)

    def scores(h, ki, buf, mask_shift):
        dst_ref, bm_ref = buf
        qt = qt_ref[h * width:(h + 1) * width, :]
        feat = lax.broadcasted_iota(jnp.int32, qt.shape, 0)
        zero = jnp.zeros_like(qt)
        q_maps = (jnp.where(feat < ATT_QK_DIM, qt, zero), jnp.where(feat >= ATT_QK_DIM, qt, zero))
        kb = k_ref[pl.ds(pl.multiple_of(ki * tk, tk), tk), h * width:(h + 1) * width]
        for j in range(2):
            s = jnp.dot(kb, q_maps[j], preferred_element_type=F32)
            if mask_shift is not None:
                kpos = lax.broadcasted_iota(jnp.int32, s.shape, 0)
                qpos = lax.broadcasted_iota(jnp.int32, s.shape, 1)
                s = jnp.where(kpos <= qpos + mask_shift, s, -jnp.inf)
            dst_ref[j] = s
            bm_ref[j] = jnp.max(s, axis=0, keepdims=True)

    def softmax_pv(h, ki, buf):
        src_ref, bm_ref = buf
        vb = vt_ref[ki, h * ATT_V_DIM:(h + 1) * ATT_V_DIM, :]
        for j in range(2):
            for c0 in range(0, tq, ATT_QCOLS):
                cols = slice(c0, c0 + ATT_QCOLS)
                m_old = m_ref[j, :, cols]
                m_new = jnp.maximum(m_old, bm_ref[j, :, cols])
                alpha = jnp.exp2(m_old - m_new)
                p = jnp.exp2(src_ref[j, :, cols] - m_new)
                l_ref[j, :, cols] = alpha * l_ref[j, :, cols] + jnp.sum(p, axis=0, keepdims=True)
                acc_ref[j, :, cols] = alpha * acc_ref[j, :, cols] + jnp.dot(
                    vb, p.astype(BF16), preferred_element_type=F32)
                m_ref[j, :, cols] = m_new

    buffers = (((sa0_ref, bma0_ref), (sb0_ref, bmb0_ref)), ((sa1_ref, bma1_ref), (sb1_ref, bmb1_ref)))
    n_heads = qt_ref.shape[0] // width
    n_pairs = jnp.maximum(qi - 1, 0) // 2
    k_rest = 2 * n_pairs

    def first_scores(h):
        if h < n_heads:
            scores(h, 0, buffers[h % 2][0], first_block_shift)

    first_scores(0)
    for h in range(n_heads):
        buf_a, buf_b = buffers[h % 2]
        m_ref[...] = jnp.full(m_ref.shape, -jnp.inf, F32)
        l_ref[...] = jnp.zeros(l_ref.shape, F32)
        acc_ref[...] = jnp.zeros(acc_ref.shape, F32)

        @pl.when(qi == 0)
        def _(h=h, buf_a=buf_a):
            softmax_pv(h, 0, buf_a)
            first_scores(h + 1)

        def pair(t, carry, h=h, buf_a=buf_a, buf_b=buf_b):
            k0 = 2 * t
            scores(h, k0 + 1, buf_b, None)
            softmax_pv(h, k0, buf_a)
            scores(h, k0 + 2, buf_a, None)
            softmax_pv(h, k0 + 1, buf_b)
            return carry

        lax.fori_loop(0, n_pairs, pair, 0)

        @pl.when(qi % 2 == 1)
        def _(h=h, buf_a=buf_a, buf_b=buf_b):
            scores(h, qi, buf_b, 0)
            softmax_pv(h, k_rest, buf_a)
            softmax_pv(h, qi, buf_b)
            first_scores(h + 1)

        @pl.when(jnp.logical_and(qi % 2 == 0, qi > 0))
        def _(h=h, buf_a=buf_a, buf_b=buf_b):
            scores(h, k_rest + 1, buf_b, None)
            softmax_pv(h, k_rest, buf_a)
            scores(h, qi, buf_a, 0)
            softmax_pv(h, k_rest + 1, buf_b)
            softmax_pv(h, qi, buf_a)
            first_scores(h + 1)

        o = acc_ref[0] * (1.0 / l_ref[0]) - (lam * (1.0 / l_ref[1])) * acc_ref[1]
        ms = jnp.mean(o * o, axis=0, keepdims=True)
        on = (o * lax.rsqrt(ms + EPS) * sg_ref[...]) * (1.0 - lam_init)
        o_ref[:, h * ATT_V_DIM:(h + 1) * ATT_V_DIM] = on.T.astype(o_ref.dtype)


def _attention(qt, k, vt, lq1, lk1, lq2, lk2, subln_g, lam_init, e, cast_jobs):
    b, s, _ = k.shape
    nk, tk = vt.shape[1], vt.shape[3]
    tq = qt.shape[3]
    nq = s // tq
    assert tq == tk

    def cast_specs(w_rows, set_rows, first_set, n_sets):
        n_steps = n_sets * CAST_STEPS_PER_SET
        assert n_steps <= b * nq
        rows = set_rows // CAST_STEPS_PER_SET
        cols = w_rows.shape[1]
        first_blk = first_set * CAST_STEPS_PER_SET

        def chunk(bi, i):
            return jnp.minimum(bi * nq + i, n_steps - 1)

        in_spec = pl.BlockSpec((rows, cols), lambda bi, i: (first_blk + chunk(bi, i), 0))
        out_spec = pl.BlockSpec((rows, cols), lambda bi, i: (chunk(bi, i), 0))
        return in_spec, out_spec, jax.ShapeDtypeStruct((n_sets * set_rows, cols), BF16)

    cast = [cast_specs(*job) for job in cast_jobs]
    scores_buf = pltpu.VMEM((2, tk, tq), F32)
    block_max = pltpu.VMEM((2, 1, tq), F32)
    return pl.pallas_call(
        functools.partial(_attn_kernel, lam_init=lam_init, n_cast=len(cast_jobs)),
        grid=(b, nq),
        in_specs=[
            pl.BlockSpec((None, None, QK_COLS, tq), lambda bi, i: (bi, i, 0, 0)),
            pl.BlockSpec((None, s, QK_COLS), lambda bi, i: (bi, 0, 0), pipeline_mode=pl.Buffered(1)),
            pl.BlockSpec((None, nk, ATT_WIDTH, tk), lambda bi, i: (bi, 0, 0, 0),
                         pipeline_mode=pl.Buffered(1)),
            _layer_spec(lq1, e), _layer_spec(lk1, e), _layer_spec(lq2, e), _layer_spec(lk2, e),
            _layer_spec(subln_g, e),
        ] + [c[0] for c in cast],
        out_specs=[pl.BlockSpec((None, tq, ATT_WIDTH), lambda bi, i: (bi, i, 0))] + [c[1] for c in cast],
        out_shape=[jax.ShapeDtypeStruct((b, s, ATT_WIDTH), BF16)] + [c[2] for c in cast],
        scratch_shapes=[
            scores_buf, scores_buf, scores_buf, scores_buf,
            block_max, block_max, block_max, block_max,
            pltpu.VMEM((2, 1, tq), F32),
            pltpu.VMEM((2, 1, tq), F32),
            pltpu.VMEM((2, ATT_V_DIM, tq), F32),
        ],
        compiler_params=_params(2),
        name="diff_attn",
    )(qt, k, vt, lq1, lk1, lq2, lk2, subln_g, *[job[0] for job in cast_jobs])


def _conv_kernel(x_ref, mod_ref, ng_ref, win_ref, cw_ref, wout_ref, out_ref, proj_ref, vh_ref):
    t = pl.program_id(1)
    tm, d = x_ref.shape
    x = x_ref[...]
    shift = mod_ref[3:4, :]
    scale = mod_ref[4:5, :]
    gate = mod_ref[5:6, :]
    h = _modulated_norm(x, ng_ref[...], scale, shift).astype(BF16)
    proj_ref[...] = jnp.dot(h, win_ref[...], preferred_element_type=F32)

    @pl.when(t == 0)
    def _():
        vh_ref[0:CONV_HIST, :] = jnp.zeros((CONV_HIST, d), F32)

    vh_ref[CONV_HIST:CONV_HIST + tm, :] = proj_ref[:, d:2 * d] * proj_ref[:, 2 * d:3 * d]
    y = cw_ref[CONV_K - 1:CONV_K, :] * vh_ref[CONV_HIST:CONV_HIST + tm, :]
    for j in range(1, CONV_K):
        y = y + cw_ref[CONV_K - 1 - j:CONV_K - j, :] * vh_ref[CONV_HIST - j:CONV_HIST - j + tm, :]
    z = (proj_ref[:, 0:d] * y).astype(BF16)
    out_ref[...] = x + gate * jnp.dot(z, wout_ref[...], preferred_element_type=F32)
    vh_ref[0:CONV_HIST, :] = vh_ref[tm:tm + CONV_HIST, :]


def _conv_mixer(x, mod, norm_g, w_in_set, conv_w, w_out_set, layer, o_idx):
    b, s, d = x.shape
    tm = CONV_TILE
    (w_in, w_in_idx), (w_out, w_out_idx) = w_in_set, w_out_set
    row = pl.BlockSpec((None, tm, d), lambda bi, i: (bi, i, 0))
    return pl.pallas_call(
        _conv_kernel,
        grid=(b, s // tm),
        in_specs=[row, _mod_spec(mod, layer), _layer_spec(norm_g, layer, 1),
                  _row_set_spec(w_in, w_in_idx, d), _layer_spec(conv_w, o_idx),
                  _row_set_spec(w_out, w_out_idx, d)],
        out_specs=row,
        out_shape=jax.ShapeDtypeStruct(x.shape, F32),
        scratch_shapes=[pltpu.VMEM((tm, 3 * d), F32), pltpu.VMEM((CONV_HIST + tm, d), F32)],
        compiler_params=_params(2),
        name="conv_mixer",
    )(x, mod, norm_g, w_in, conv_w, w_out)


def kernel(x, c, norm_g, w_ada, b_ada, ffn_wg, ffn_wu, ffn_wd, w_in_ab, qk_norm_g, lambda_q1, lambda_k1,
           lambda_q2, lambda_k2, subln_g, pool_w, pool_scale, w_out_ab, w_in_c, conv_w, w_out_c):
    depth = norm_g.shape[0]
    d = x.shape[-1]
    n_even = w_in_ab.shape[0]
    mod = _ada_modulation(c, w_ada, b_ada)

    n_sets_total = 2 * depth
    n_odd = w_in_c.shape[0]
    d_ff = ffn_wg.shape[-1]

    def rows2d(w):
        return w.reshape(-1, w.shape[-1])

    ffn_sets = {0: ((ffn_wg[0, 0].astype(BF16), ffn_wu[0, 0].astype(BF16), ffn_wd[0, 0].astype(BF16)), 0)}
    w_in_ab_sets = {0: (w_in_ab[0].astype(BF16), 0)}
    w_out_ab_sets, w_in_c_sets, w_out_c_sets = {}, {}, {}
    pool_w16 = pool_w.astype(BF16)
    grp = jnp.arange(V7X_MXU_DIM) // ATT_QK_DIM
    seg = (grp[:, None] == grp[None, :]).astype(BF16)
    qk_gain = jnp.tile(qk_norm_g, (1, 1, QK_COLS // ATT_QK_DIM))
    norm_g4 = norm_g.reshape(depth, 3, 1, d)
    pool_scale3 = pool_scale.reshape(n_even, 1, POOL_WIDTH)
    lam_rows = [a.reshape(n_even, 1, ATT_QK_DIM) for a in (lambda_q1, lambda_k1, lambda_q2, lambda_k2)]
    subln_col = subln_g.reshape(n_even, ATT_V_DIM, 1)

    for l in range(depth):
        x = _ffn(x, mod, norm_g4, ffn_sets[2 * l], l, 0)
        if l % 2 == 0:
            e = l // 2
            lam_init = 0.8 - 0.6 * math.exp(-0.3 * l)
            qt, k, vt, p = _inproj(x, mod, norm_g4, w_in_ab_sets[e], qk_gain, seg, pool_w16, pool_scale3, l, e)
            first_set = 2 * l + 1
            n_sets = min(4, n_sets_total - first_set)
            jobs = [(rows2d(ffn_wg), d, first_set, n_sets), (rows2d(ffn_wu), d, first_set, n_sets),
                    (rows2d(ffn_wd), d_ff, first_set, n_sets), (rows2d(w_out_ab), d, e, 1)]
            if e < n_odd:
                jobs += [(rows2d(w_in_c), d, e, 1), (rows2d(w_out_c), d, e, 1)]
            if e + 1 < n_even:
                jobs += [(rows2d(w_in_ab), d, e + 1, 1)]
            o, *cast_w = _attention(qt, k, vt, *lam_rows, subln_col, lam_init, e, jobs)
            for j in range(n_sets):
                ffn_sets[first_set + j] = (tuple(cast_w[0:3]), j)
            w_out_ab_sets[e] = (cast_w[3], 0)
            if e < n_odd:
                w_in_c_sets[e], w_out_c_sets[e] = (cast_w[4], 0), (cast_w[5], 0)
            if e + 1 < n_even:
                w_in_ab_sets[e + 1] = (cast_w[-1], 0)
            x = _ffn(x, mod, norm_g4, ffn_sets[2 * l + 1], l, 1, mixer=(o, p, w_out_ab_sets[e]))
        else:
            o_idx = l // 2
            x = _conv_mixer(x, mod, norm_g4, w_in_c_sets[o_idx], conv_w, w_out_c_sets[o_idx], l, o_idx)
            x = _ffn(x, mod, norm_g4, ffn_sets[2 * l + 1], l, 1)
    return x
```

```python
import functools
import math

import jax
import jax.numpy as jnp
from jax import lax
from jax.experimental import pallas as pl
from jax.experimental.pallas import tpu as pltpu

F32 = jnp.float32
BF16 = jnp.bfloat16

EPS = 1e-6
N_MOD = 9
ATT_HEADS = 4
ATT_QK_DIM = 64
ATT_V_DIM = 128
QK_COLS = 512
ATT_WIDTH = 512
POOL_WINDOWS = (2, 4, 8, 16)
POOL_WIDTH = 512
POOL_GROUP_DIM = 128
CONV_K = 3

V7X_MXU_DIM = 256
VMEM_LIMIT_BYTES = 56 * 1024 * 1024

ROW_TILE = 512
FFN_TILE = 1024
CONV_TILE = 1024
ATT_QCOLS = V7X_MXU_DIM
FF_CHUNK = V7X_MXU_DIM
CAST_STEPS_PER_SET = 8
POOL_HIST = 8 * len(POOL_WINDOWS)
assert POOL_WINDOWS == tuple(2 ** (g + 1) for g in range(len(POOL_WINDOWS))) and POOL_HIST >= POOL_WINDOWS[-1]
CONV_HIST = 8
LOG2E = math.log2(math.e)


def _params(n_axes):
    return pltpu.CompilerParams(
        dimension_semantics=("arbitrary",) * n_axes, vmem_limit_bytes=VMEM_LIMIT_BYTES)


def _layer_spec(arr, *lead):
    tail = arr.shape[len(lead):]
    block = (None,) * len(lead) + tail
    index = tuple(lead) + (0,) * len(tail)
    return pl.BlockSpec(block, lambda *_: index, pipeline_mode=pl.Buffered(1))


def _modulated_norm(x, g, scale, shift):
    ms = jnp.mean(x * x, axis=-1, keepdims=True)
    y = x * lax.rsqrt(ms + EPS)
    return (y * g) * (1.0 + scale) + shift


def _silu(x):
    return x * jax.nn.sigmoid(x)


def _ada_kernel(c_ref, w_ref, b_ref, o_ref):
    c_act = _silu(c_ref[...]).astype(BF16)
    w = w_ref[...].astype(BF16)
    o_ref[...] = jnp.dot(c_act, w, preferred_element_type=F32) + b_ref[...]


def _ada_modulation(c, w_ada, b_ada):
    depth, d, _ = w_ada.shape
    b = c.shape[0]
    rows = 8
    c_pad = jnp.zeros((rows, d), F32).at[:b].set(c)
    out = pl.pallas_call(
        _ada_kernel,
        grid=(depth, N_MOD),
        in_specs=[
            pl.BlockSpec((rows, d), lambda l, j: (0, 0)),
            pl.BlockSpec((None, d, d), lambda l, j: (l, 0, j)),
            pl.BlockSpec((None, None, 1, d), lambda l, j: (l, j, 0, 0)),
        ],
        out_specs=pl.BlockSpec((None, rows, d), lambda l, j: (l, 0, j)),
        out_shape=jax.ShapeDtypeStruct((depth, rows, N_MOD * d), F32),
        compiler_params=_params(2),
        name="ada_mod",
    )(c_pad, w_ada, b_ada.reshape(depth, N_MOD, 1, d))
    return out[:, :b].reshape(depth, b, N_MOD, d)


def _mod_spec(mod, layer):
    d = mod.shape[-1]
    return pl.BlockSpec((None, None, N_MOD, d), lambda bi, i: (layer, bi, 0, 0))


def _ffn_kernel(x_ref, mod_ref, ng_ref, wg_ref, wu_ref, wd_ref, *rest, sub, mix):
    if mix:
        ao_ref, po_ref, wo_ref, o_ref, a_ref = rest
        cat = jnp.concatenate([ao_ref[...], po_ref[...]], axis=-1)
        x = x_ref[...] + mod_ref[5:6, :] * jnp.dot(cat, wo_ref[...], preferred_element_type=F32)
    else:
        o_ref, a_ref = rest
        x = x_ref[...]
    shift = mod_ref[3 * sub:3 * sub + 1, :]
    scale = mod_ref[3 * sub + 1:3 * sub + 2, :]
    gate = mod_ref[3 * sub + 2:3 * sub + 3, :]
    h = _modulated_norm(x, ng_ref[...], scale, shift).astype(BF16)
    d_ff = a_ref.shape[1]
    for c0 in range(0, d_ff, FF_CHUNK):
        c1 = min(c0 + FF_CHUNK, d_ff)
        g = jnp.dot(h, wg_ref[:, c0:c1], preferred_element_type=F32)
        u = jnp.dot(h, wu_ref[:, c0:c1], preferred_element_type=F32)
        a_ref[:, c0:c1] = (_silu(g) * u).astype(BF16)
    y = jnp.dot(a_ref[...], wd_ref[...], preferred_element_type=F32)
    o_ref[...] = x + (0.5 * gate) * y


def _row_set_spec(w_rows, idx, set_rows):
    return pl.BlockSpec((set_rows, w_rows.shape[1]), lambda *_: (idx, 0), pipeline_mode=pl.Buffered(1))


def _ffn(x, mod, norm_g, weights, layer, half, mixer=None):
    b, s, d = x.shape
    (wg, wu, wd), w_idx = weights
    d_ff = wg.shape[1]
    tm = FFN_TILE
    sub = 2 * half
    row_spec = pl.BlockSpec((None, tm, d), lambda bi, i: (bi, i, 0))
    in_specs = [
        row_spec,
        _mod_spec(mod, layer),
        _layer_spec(norm_g, layer, sub),
        _row_set_spec(wg, w_idx, d),
        _row_set_spec(wu, w_idx, d),
        _row_set_spec(wd, w_idx, d_ff),
    ]
    args = [x, mod, norm_g, wg, wu, wd]
    if mixer is not None:
        ao, po, (w_out, w_out_idx) = mixer
        half_spec = pl.BlockSpec((None, tm, ATT_WIDTH), lambda bi, i: (bi, i, 0))
        in_specs += [half_spec, half_spec, _row_set_spec(w_out, w_out_idx, d)]
        args += [ao, po, w_out]
    return pl.pallas_call(
        functools.partial(_ffn_kernel, sub=sub, mix=mixer is not None),
        grid=(b, s // tm),
        in_specs=in_specs,
        out_specs=row_spec,
        out_shape=jax.ShapeDtypeStruct(x.shape, F32),
        scratch_shapes=[pltpu.VMEM((tm, d_ff), BF16)],
        compiler_params=_params(2),
        name=f"ffn{sub}" + ("_mix" if mixer is not None else ""),
    )(*args)


def _group_mean_sq(xc, seg):
    sq = xc * xc
    hi = sq.astype(BF16)
    lo = (sq - hi.astype(F32)).astype(BF16)
    return jnp.dot(hi, seg, preferred_element_type=F32) + jnp.dot(lo, seg, preferred_element_type=F32)


def _inproj_kernel(x_ref, mod_ref, ng_ref, win_ref, qkg_ref, seg_ref, pw_ref, ps_ref,
                   qt_ref, k_ref, vt_ref, p_ref, proj_ref, uh_ref, la_ref, lb_ref):
    t = pl.program_id(1)
    tm = x_ref.shape[0]
    x = x_ref[...]
    shift = mod_ref[3:4, :]
    scale = mod_ref[4:5, :]
    h = _modulated_norm(x, ng_ref[...], scale, shift).astype(BF16)
    proj_ref[...] = jnp.dot(h, win_ref[...], preferred_element_type=F32)

    seg = seg_ref[...]
    q_scale = ATT_QK_DIM ** -0.5 * LOG2E
    for which in range(2):
        gain = qkg_ref[which:which + 1, :]
        if which == 0:
            gain = gain * q_scale
        for c0 in range(0, QK_COLS, V7X_MXU_DIM):
            xc = proj_ref[:, which * QK_COLS + c0:which * QK_COLS + c0 + V7X_MXU_DIM]
            ms = _group_mean_sq(xc, seg)
            normed = xc * lax.rsqrt(ms + EPS) * gain[:, c0:c0 + V7X_MXU_DIM]
            if which == 0:
                qt_ref[c0:c0 + V7X_MXU_DIM, :] = normed.T.astype(BF16)
            else:
                k_ref[:, c0:c0 + V7X_MXU_DIM] = normed.astype(BF16)

    v = proj_ref[:, 2 * QK_COLS:2 * QK_COLS + ATT_WIDTH]
    vt_ref[...] = v.T.astype(BF16)

    u0 = 2 * QK_COLS + ATT_WIDTH

    @pl.when(t == 0)
    def _():
        uh_ref[0:POOL_HIST, :] = jnp.zeros((POOL_HIST, POOL_WIDTH), F32)

    uh_ref[POOL_HIST:POOL_HIST + tm, :] = proj_ref[:, u0:u0 + POOL_WIDTH]
    end = POOL_HIST + tm
    src_ref = uh_ref
    for j in range(1, len(POOL_WINDOWS) + 1):
        dst_ref = (la_ref, lb_ref)[(j - 1) % 2]
        lo, sh, c_lo = 8 * j, 2 ** (j - 1), (j - 1) * POOL_GROUP_DIM
        dst_ref[lo:end, c_lo:] = src_ref[lo:end, c_lo:] + src_ref[lo - sh:end - sh, c_lo:]
        src_ref = dst_ref
    pos = t * tm + lax.broadcasted_iota(jnp.int32, (tm, 1), 0)
    for g, w in enumerate(POOL_WINDOWS):
        cols = slice(g * POOL_GROUP_DIM, (g + 1) * POOL_GROUP_DIM)
        cur = uh_ref[POOL_HIST:end, cols]
        acc = (la_ref, lb_ref)[g % 2][POOL_HIST:end, cols]
        cnt = jnp.minimum(pos + 1, w).astype(F32)
        dlt = (acc / cnt - cur).astype(BF16)
        y = jnp.dot(dlt, pw_ref[g], preferred_element_type=F32)
        p_ref[:, cols] = (y * ps_ref[:, cols]).astype(BF16)
    uh_ref[0:POOL_HIST, :] = uh_ref[tm:end, :]


def _inproj(x, mod, norm_g, w_in_set, qk_gain, seg, pool_w, pool_scale, layer, e):
    b, s, d = x.shape
    tm = ROW_TILE
    w_in, w_in_idx = w_in_set
    width = w_in.shape[-1]
    row_in = pl.BlockSpec((None, tm, d), lambda bi, i: (bi, i, 0))
    row_out = pl.BlockSpec((None, tm, QK_COLS), lambda bi, i: (bi, i, 0))
    return pl.pallas_call(
        _inproj_kernel,
        grid=(b, s // tm),
        in_specs=[
            row_in,
            _mod_spec(mod, layer),
            _layer_spec(norm_g, layer, 1),
            _row_set_spec(w_in, w_in_idx, d),
            _layer_spec(qk_gain, e),
            _layer_spec(seg),
            _layer_spec(pool_w, e),
            _layer_spec(pool_scale, e),
        ],
        out_specs=[
            pl.BlockSpec((None, None, QK_COLS, tm), lambda bi, i: (bi, i, 0, 0)),
            row_out,
            pl.BlockSpec((None, None, ATT_WIDTH, tm), lambda bi, i: (bi, i, 0, 0)),
            row_out,
        ],
        out_shape=[
            jax.ShapeDtypeStruct((b, s // tm, QK_COLS, tm), BF16),
            jax.ShapeDtypeStruct((b, s, QK_COLS), BF16),
            jax.ShapeDtypeStruct((b, s // tm, ATT_WIDTH, tm), BF16),
            jax.ShapeDtypeStruct((b, s, POOL_WIDTH), BF16),
        ],
        scratch_shapes=[pltpu.VMEM((tm, width), F32)] + [pltpu.VMEM((POOL_HIST + tm, POOL_WIDTH), F32)] * 3,
        compiler_params=_params(2),
        name="inproj_ab",
    )(x, mod, norm_g, w_in, qk_gain, seg, pool_w, pool_scale)


def _attn_kernel(qt_ref, k_ref, vt_ref, lq1_ref, lk1_ref, lq2_ref, lk2_ref, sg_ref, *rest, lam_init, n_cast):
    cast_src = rest[:n_cast]
    o_ref = rest[n_cast]
    cast_dst = rest[n_cast + 1:2 * n_cast + 1]
    (sa0_ref, sb0_ref, sa1_ref, sb1_ref, bma0_ref, bmb0_ref, bma1_ref, bmb1_ref,
     m_ref, l_ref, acc_ref) = rest[2 * n_cast + 1:]
    for src_ref, dst_ref in zip(cast_src, cast_dst):
        dst_ref[...] = src_ref[...].astype(BF16)

    qi = pl.program_id(1)
    tq = qt_ref.shape[1]
    tk = vt_ref.shape[2]
    width = 2 * ATT_QK_DIM
    first_block_shift = jnp.where(qi == 0, 0, tk)
    lam = (jnp.exp(jnp.sum(lq1_ref[...] * lk1_ref[...], axis=-1, keepdims=True))
           - jnp.exp(jnp.sum(lq2_ref[...] * lk2_ref[...], axis=-1, keepdims=True))
           + lam_init)

    def visible_rows(c0, diagonal):
        return min(c0 + ATT_QCOLS, tk) if diagonal else tk

    def scores(h, ki, buf, mask_shift=None, diagonal=False):
        dst_ref, bm_ref = buf
        qt = qt_ref[h * width:(h + 1) * width, :]
        feat = lax.broadcasted_iota(jnp.int32, qt.shape, 0)
        zero = jnp.zeros_like(qt)
        q_maps = (jnp.where(feat < ATT_QK_DIM, qt, zero), jnp.where(feat >= ATT_QK_DIM, qt, zero))
        kb = k_ref[pl.ds(pl.multiple_of(ki * tk, tk), tk), h * width:(h + 1) * width]
        for j in range(2):
            if diagonal:
                for c0 in range(0, tq, ATT_QCOLS):
                    cols = slice(c0, c0 + ATT_QCOLS)
                    rows = visible_rows(c0, True)
                    s = jnp.dot(kb[0:rows, :], q_maps[j][:, cols], preferred_element_type=F32)
                    kpos = lax.broadcasted_iota(jnp.int32, s.shape, 0)
                    qpos = lax.broadcasted_iota(jnp.int32, s.shape, 1) + c0
                    s = jnp.where(kpos <= qpos, s, -jnp.inf)
                    dst_ref[j, 0:rows, cols] = s
                    bm_ref[j, :, cols] = jnp.max(s, axis=0, keepdims=True)
            else:
                s = jnp.dot(kb, q_maps[j], preferred_element_type=F32)
                if mask_shift is not None:
                    kpos = lax.broadcasted_iota(jnp.int32, s.shape, 0)
                    qpos = lax.broadcasted_iota(jnp.int32, s.shape, 1)
                    s = jnp.where(kpos <= qpos + mask_shift, s, -jnp.inf)
                dst_ref[j] = s
                bm_ref[j] = jnp.max(s, axis=0, keepdims=True)

    def softmax_pv(h, ki, buf, diagonal=False):
        src_ref, bm_ref = buf
        vb = vt_ref[ki, h * ATT_V_DIM:(h + 1) * ATT_V_DIM, :]
        for j in range(2):
            for c0 in range(0, tq, ATT_QCOLS):
                cols = slice(c0, c0 + ATT_QCOLS)
                rows = visible_rows(c0, diagonal)
                m_old = m_ref[j, :, cols]
                m_new = jnp.maximum(m_old, bm_ref[j, :, cols])
                alpha = jnp.exp2(m_old - m_new)
                p = jnp.exp2(src_ref[j, 0:rows, cols] - m_new)
                l_ref[j, :, cols] = alpha * l_ref[j, :, cols] + jnp.sum(p, axis=0, keepdims=True)
                acc_ref[j, :, cols] = alpha * acc_ref[j, :, cols] + jnp.dot(
                    vb[:, 0:rows], p.astype(BF16), preferred_element_type=F32)
                m_ref[j, :, cols] = m_new

    buffers = (((sa0_ref, bma0_ref), (sb0_ref, bmb0_ref)), ((sa1_ref, bma1_ref), (sb1_ref, bmb1_ref)))
    n_heads = qt_ref.shape[0] // width
    n_pairs = jnp.maximum(qi - 1, 0) // 2
    k_rest = 2 * n_pairs

    def first_scores(h):
        if h < n_heads:
            scores(h, 0, buffers[h % 2][0], mask_shift=first_block_shift)

    first_scores(0)
    for h in range(n_heads):
        buf_a, buf_b = buffers[h % 2]
        m_ref[...] = jnp.full(m_ref.shape, -jnp.inf, F32)
        l_ref[...] = jnp.zeros(l_ref.shape, F32)
        acc_ref[...] = jnp.zeros(acc_ref.shape, F32)

        @pl.when(qi == 0)
        def _(h=h, buf_a=buf_a):
            softmax_pv(h, 0, buf_a)
            first_scores(h + 1)

        def pair(t, carry, h=h, buf_a=buf_a, buf_b=buf_b):
            k0 = 2 * t
            scores(h, k0 + 1, buf_b)
            softmax_pv(h, k0, buf_a)
            scores(h, k0 + 2, buf_a)
            softmax_pv(h, k0 + 1, buf_b)
            return carry

        lax.fori_loop(0, n_pairs, pair, 0)

        @pl.when(qi % 2 == 1)
        def _(h=h, buf_a=buf_a, buf_b=buf_b):
            scores(h, qi, buf_b, diagonal=True)
            softmax_pv(h, k_rest, buf_a)
            softmax_pv(h, qi, buf_b, diagonal=True)
            first_scores(h + 1)

        @pl.when(jnp.logical_and(qi % 2 == 0, qi > 0))
        def _(h=h, buf_a=buf_a, buf_b=buf_b):
            scores(h, k_rest + 1, buf_b)
            softmax_pv(h, k_rest, buf_a)
            scores(h, qi, buf_a, diagonal=True)
            softmax_pv(h, k_rest + 1, buf_b)
            softmax_pv(h, qi, buf_a, diagonal=True)
            first_scores(h + 1)

        o = acc_ref[0] * (1.0 / l_ref[0]) - (lam * (1.0 / l_ref[1])) * acc_ref[1]
        ms = jnp.mean(o * o, axis=0, keepdims=True)
        on = (o * lax.rsqrt(ms + EPS) * sg_ref[...]) * (1.0 - lam_init)
        o_ref[:, h * ATT_V_DIM:(h + 1) * ATT_V_DIM] = on.T.astype(o_ref.dtype)


def _attention(qt, k, vt, lq1, lk1, lq2, lk2, subln_g, lam_init, e, cast_jobs):
    b, s, _ = k.shape
    nk, tk = vt.shape[1], vt.shape[3]
    tq = qt.shape[3]
    nq = s // tq
    assert tq == tk

    def cast_specs(w_rows, set_rows, first_set, n_sets):
        n_steps = n_sets * CAST_STEPS_PER_SET
        assert n_steps <= b * nq
        rows = set_rows // CAST_STEPS_PER_SET
        cols = w_rows.shape[1]
        first_blk = first_set * CAST_STEPS_PER_SET

        def chunk(bi, i):
            return jnp.minimum(bi * nq + i, n_steps - 1)

        in_spec = pl.BlockSpec((rows, cols), lambda bi, i: (first_blk + chunk(bi, i), 0))
        out_spec = pl.BlockSpec((rows, cols), lambda bi, i: (chunk(bi, i), 0))
        return in_spec, out_spec, jax.ShapeDtypeStruct((n_sets * set_rows, cols), BF16)

    cast = [cast_specs(*job) for job in cast_jobs]
    scores_buf = pltpu.VMEM((2, tk, tq), F32)
    block_max = pltpu.VMEM((2, 1, tq), F32)
    return pl.pallas_call(
        functools.partial(_attn_kernel, lam_init=lam_init, n_cast=len(cast_jobs)),
        grid=(b, nq),
        in_specs=[
            pl.BlockSpec((None, None, QK_COLS, tq), lambda bi, i: (bi, i, 0, 0)),
            pl.BlockSpec((None, s, QK_COLS), lambda bi, i: (bi, 0, 0), pipeline_mode=pl.Buffered(1)),
            pl.BlockSpec((None, nk, ATT_WIDTH, tk), lambda bi, i: (bi, 0, 0, 0),
                         pipeline_mode=pl.Buffered(1)),
            _layer_spec(lq1, e), _layer_spec(lk1, e), _layer_spec(lq2, e), _layer_spec(lk2, e),
            _layer_spec(subln_g, e),
        ] + [c[0] for c in cast],
        out_specs=[pl.BlockSpec((None, tq, ATT_WIDTH), lambda bi, i: (bi, i, 0))] + [c[1] for c in cast],
        out_shape=[jax.ShapeDtypeStruct((b, s, ATT_WIDTH), BF16)] + [c[2] for c in cast],
        scratch_shapes=[
            scores_buf, scores_buf, scores_buf, scores_buf,
            block_max, block_max, block_max, block_max,
            pltpu.VMEM((2, 1, tq), F32),
            pltpu.VMEM((2, 1, tq), F32),
            pltpu.VMEM((2, ATT_V_DIM, tq), F32),
        ],
        compiler_params=_params(2),
        name="diff_attn",
    )(qt, k, vt, lq1, lk1, lq2, lk2, subln_g, *[job[0] for job in cast_jobs])


def _conv_kernel(x_ref, mod_ref, ng_ref, win_ref, cw_ref, wout_ref, out_ref, proj_ref, vh_ref):
    t = pl.program_id(1)
    tm, d = x_ref.shape
    x = x_ref[...]
    shift = mod_ref[3:4, :]
    scale = mod_ref[4:5, :]
    gate = mod_ref[5:6, :]
    h = _modulated_norm(x, ng_ref[...], scale, shift).astype(BF16)
    proj_ref[...] = jnp.dot(h, win_ref[...], preferred_element_type=F32)

    @pl.when(t == 0)
    def _():
        vh_ref[0:CONV_HIST, :] = jnp.zeros((CONV_HIST, d), F32)

    vh_ref[CONV_HIST:CONV_HIST + tm, :] = proj_ref[:, d:2 * d] * proj_ref[:, 2 * d:3 * d]
    y = cw_ref[CONV_K - 1:CONV_K, :] * vh_ref[CONV_HIST:CONV_HIST + tm, :]
    for j in range(1, CONV_K):
        y = y + cw_ref[CONV_K - 1 - j:CONV_K - j, :] * vh_ref[CONV_HIST - j:CONV_HIST - j + tm, :]
    z = (proj_ref[:, 0:d] * y).astype(BF16)
    out_ref[...] = x + gate * jnp.dot(z, wout_ref[...], preferred_element_type=F32)
    vh_ref[0:CONV_HIST, :] = vh_ref[tm:tm + CONV_HIST, :]


def _conv_mixer(x, mod, norm_g, w_in_set, conv_w, w_out_set, layer, o_idx):
    b, s, d = x.shape
    tm = CONV_TILE
    (w_in, w_in_idx), (w_out, w_out_idx) = w_in_set, w_out_set
    row = pl.BlockSpec((None, tm, d), lambda bi, i: (bi, i, 0))
    return pl.pallas_call(
        _conv_kernel,
        grid=(b, s // tm),
        in_specs=[row, _mod_spec(mod, layer), _layer_spec(norm_g, layer, 1),
                  _row_set_spec(w_in, w_in_idx, d), _layer_spec(conv_w, o_idx),
                  _row_set_spec(w_out, w_out_idx, d)],
        out_specs=row,
        out_shape=jax.ShapeDtypeStruct(x.shape, F32),
        scratch_shapes=[pltpu.VMEM((tm, 3 * d), F32), pltpu.VMEM((CONV_HIST + tm, d), F32)],
        compiler_params=_params(2),
        name="conv_mixer",
    )(x, mod, norm_g, w_in, conv_w, w_out)


def kernel(x, c, norm_g, w_ada, b_ada, ffn_wg, ffn_wu, ffn_wd, w_in_ab, qk_norm_g, lambda_q1, lambda_k1,
           lambda_q2, lambda_k2, subln_g, pool_w, pool_scale, w_out_ab, w_in_c, conv_w, w_out_c):
    depth = norm_g.shape[0]
    d = x.shape[-1]
    n_even = w_in_ab.shape[0]
    mod = _ada_modulation(c, w_ada, b_ada)

    n_sets_total = 2 * depth
    n_odd = w_in_c.shape[0]
    d_ff = ffn_wg.shape[-1]

    def rows2d(w):
        return w.reshape(-1, w.shape[-1])

    ffn_sets = {0: ((ffn_wg[0, 0].astype(BF16), ffn_wu[0, 0].astype(BF16), ffn_wd[0, 0].astype(BF16)), 0)}
    w_in_ab_sets = {0: (w_in_ab[0].astype(BF16), 0)}
    w_out_ab_sets, w_in_c_sets, w_out_c_sets = {}, {}, {}
    pool_w16 = pool_w.astype(BF16)
    grp = jnp.arange(V7X_MXU_DIM) // ATT_QK_DIM
    seg = ((grp[:, None] == grp[None, :]).astype(F32) / ATT_QK_DIM).astype(BF16)
    qk_gain = jnp.tile(qk_norm_g, (1, 1, QK_COLS // ATT_QK_DIM))
    norm_g4 = norm_g.reshape(depth, 3, 1, d)
    pool_scale3 = pool_scale.reshape(n_even, 1, POOL_WIDTH)
    lam_rows = [a.reshape(n_even, 1, ATT_QK_DIM) for a in (lambda_q1, lambda_k1, lambda_q2, lambda_k2)]
    subln_col = subln_g.reshape(n_even, ATT_V_DIM, 1)

    for l in range(depth):
        x = _ffn(x, mod, norm_g4, ffn_sets[2 * l], l, 0)
        if l % 2 == 0:
            e = l // 2
            lam_init = 0.8 - 0.6 * math.exp(-0.3 * l)
            qt, k, vt, p = _inproj(x, mod, norm_g4, w_in_ab_sets[e], qk_gain, seg, pool_w16, pool_scale3, l, e)
            first_set = 2 * l + 1
            n_sets = min(4, n_sets_total - first_set)
            jobs = [(rows2d(ffn_wg), d, first_set, n_sets), (rows2d(ffn_wu), d, first_set, n_sets),
                    (rows2d(ffn_wd), d_ff, first_set, n_sets), (rows2d(w_out_ab), d, e, 1)]
            if e < n_odd:
                jobs += [(rows2d(w_in_c), d, e, 1), (rows2d(w_out_c), d, e, 1)]
            if e + 1 < n_even:
                jobs += [(rows2d(w_in_ab), d, e + 1, 1)]
            o, *cast_w = _attention(qt, k, vt, *lam_rows, subln_col, lam_init, e, jobs)
            for j in range(n_sets):
                ffn_sets[first_set + j] = (tuple(cast_w[0:3]), j)
            w_out_ab_sets[e] = (cast_w[3], 0)
            if e < n_odd:
                w_in_c_sets[e], w_out_c_sets[e] = (cast_w[4], 0), (cast_w[5], 0)
            if e + 1 < n_even:
                w_in_ab_sets[e + 1] = (cast_w[-1], 0)
            x = _ffn(x, mod, norm_g4, ffn_sets[2 * l + 1], l, 1, mixer=(o, p, w_out_ab_sets[e]))
        else:
            o_idx = l // 2
            x = _conv_mixer(x, mod, norm_g4, w_in_c_sets[o_idx], conv_w, w_out_c_sets[o_idx], l, o_idx)
            x = _ffn(x, mod, norm_g4, ffn_sets[2 * l + 1], l, 1)
    return x
```

```python
import functools
import math

import jax
import jax.numpy as jnp
from jax import lax
from jax.experimental import pallas as pl
from jax.experimental.pallas import tpu as pltpu

F32 = jnp.float32
BF16 = jnp.bfloat16

EPS = 1e-6
N_MOD = 9
ATT_HEADS = 4
ATT_QK_DIM = 64
ATT_V_DIM = 128
QK_COLS = 512
ATT_WIDTH = 512
POOL_WINDOWS = (2, 4, 8, 16)
POOL_WIDTH = 512
POOL_GROUP_DIM = 128
CONV_K = 3

V7X_MXU_DIM = 256
VMEM_LIMIT_BYTES = 56 * 1024 * 1024

ROW_TILE = 512
FFN_TILE = 1024
CONV_TILE = 1024
ATT_QCOLS = V7X_MXU_DIM
FF_CHUNK = V7X_MXU_DIM
CAST_STEPS_PER_SET = 8
POOL_HIST = 8 * len(POOL_WINDOWS)
assert POOL_WINDOWS == tuple(2 ** (g + 1) for g in range(len(POOL_WINDOWS))) and POOL_HIST >= POOL_WINDOWS[-1]
CONV_HIST = 8
LOG2E = math.log2(math.e)


def _params(n_axes):
    return pltpu.CompilerParams(
        dimension_semantics=("arbitrary",) * n_axes, vmem_limit_bytes=VMEM_LIMIT_BYTES)


def _layer_spec(arr, *lead):
    tail = arr.shape[len(lead):]
    block = (None,) * len(lead) + tail
    index = tuple(lead) + (0,) * len(tail)
    return pl.BlockSpec(block, lambda *_: index, pipeline_mode=pl.Buffered(1))


def _modulated_norm(x, g, scale, shift):
    ms = jnp.mean(x * x, axis=-1, keepdims=True)
    y = x * lax.rsqrt(ms + EPS)
    return (y * g) * (1.0 + scale) + shift


def _silu(x):
    return x * jax.nn.sigmoid(x)


def _ada_kernel(c_ref, w_ref, b_ref, o_ref):
    c_act = _silu(c_ref[...]).astype(BF16)
    w = w_ref[...].astype(BF16)
    o_ref[...] = jnp.dot(c_act, w, preferred_element_type=F32) + b_ref[...]


def _ada_modulation(c, w_ada, b_ada):
    depth, d, _ = w_ada.shape
    b = c.shape[0]
    rows = 8
    c_pad = jnp.zeros((rows, d), F32).at[:b].set(c)
    out = pl.pallas_call(
        _ada_kernel,
        grid=(depth, N_MOD),
        in_specs=[
            pl.BlockSpec((rows, d), lambda l, j: (0, 0)),
            pl.BlockSpec((None, d, d), lambda l, j: (l, 0, j)),
            pl.BlockSpec((None, None, 1, d), lambda l, j: (l, j, 0, 0)),
        ],
        out_specs=pl.BlockSpec((None, rows, d), lambda l, j: (l, 0, j)),
        out_shape=jax.ShapeDtypeStruct((depth, rows, N_MOD * d), F32),
        compiler_params=_params(2),
        name="ada_mod",
    )(c_pad, w_ada, b_ada.reshape(depth, N_MOD, 1, d))
    return out[:, :b].reshape(depth, b, N_MOD, d)


def _mod_spec(mod, layer):
    d = mod.shape[-1]
    return pl.BlockSpec((None, None, N_MOD, d), lambda bi, i: (layer, bi, 0, 0))


def _ffn_kernel(x_ref, mod_ref, ng_ref, wg_ref, wu_ref, wd_ref, *rest, sub, mix):
    if mix:
        ao_ref, po_ref, wo_ref, o_ref, a_ref = rest
        cat = jnp.concatenate([ao_ref[...], po_ref[...]], axis=-1)
        x = x_ref[...] + mod_ref[5:6, :] * jnp.dot(cat, wo_ref[...], preferred_element_type=F32)
    else:
        o_ref, a_ref = rest
        x = x_ref[...]
    shift = mod_ref[3 * sub:3 * sub + 1, :]
    scale = mod_ref[3 * sub + 1:3 * sub + 2, :]
    gate = mod_ref[3 * sub + 2:3 * sub + 3, :]
    h = _modulated_norm(x, ng_ref[...], scale, shift).astype(BF16)
    d_ff = a_ref.shape[1]
    for c0 in range(0, d_ff, FF_CHUNK):
        c1 = min(c0 + FF_CHUNK, d_ff)
        g = jnp.dot(h, wg_ref[:, c0:c1], preferred_element_type=F32)
        u = jnp.dot(h, wu_ref[:, c0:c1], preferred_element_type=F32)
        a_ref[:, c0:c1] = (_silu(g) * u).astype(BF16)
    y = jnp.dot(a_ref[...], wd_ref[...], preferred_element_type=F32)
    o_ref[...] = x + (0.5 * gate) * y


def _row_set_spec(w_rows, idx, set_rows):
    return pl.BlockSpec((set_rows, w_rows.shape[1]), lambda *_: (idx, 0), pipeline_mode=pl.Buffered(1))


def _ffn(x, mod, norm_g, weights, layer, half, mixer=None):
    b, s, d = x.shape
    (wg, wu, wd), w_idx = weights
    d_ff = wg.shape[1]
    tm = FFN_TILE
    sub = 2 * half
    row_spec = pl.BlockSpec((None, tm, d), lambda bi, i: (bi, i, 0))
    in_specs = [
        row_spec,
        _mod_spec(mod, layer),
        _layer_spec(norm_g, layer, sub),
        _row_set_spec(wg, w_idx, d),
        _row_set_spec(wu, w_idx, d),
        _row_set_spec(wd, w_idx, d_ff),
    ]
    args = [x, mod, norm_g, wg, wu, wd]
    if mixer is not None:
        ao, po, (w_out, w_out_idx) = mixer
        half_spec = pl.BlockSpec((None, tm, ATT_WIDTH), lambda bi, i: (bi, i, 0))
        in_specs += [half_spec, half_spec, _row_set_spec(w_out, w_out_idx, d)]
        args += [ao, po, w_out]
    return pl.pallas_call(
        functools.partial(_ffn_kernel, sub=sub, mix=mixer is not None),
        grid=(b, s // tm),
        in_specs=in_specs,
        out_specs=row_spec,
        out_shape=jax.ShapeDtypeStruct(x.shape, F32),
        scratch_shapes=[pltpu.VMEM((tm, d_ff), BF16)],
        compiler_params=_params(2),
        name=f"ffn{sub}" + ("_mix" if mixer is not None else ""),
    )(*args)


def _group_mean_sq(xc, seg):
    sq = xc * xc
    hi = sq.astype(BF16)
    lo = (sq - hi.astype(F32)).astype(BF16)
    return jnp.dot(hi, seg, preferred_element_type=F32) + jnp.dot(lo, seg, preferred_element_type=F32)


def _inproj_kernel(x_ref, mod_ref, ng_ref, win_ref, qkg_ref, seg_ref, pw_ref, ps_ref,
                   qt_ref, k_ref, vt_ref, p_ref, proj_ref, uh_ref, *lv_refs):
    t = pl.program_id(1)
    tm = x_ref.shape[0]

    @pl.when(t == 0)
    def _():
        uh_ref[0:POOL_HIST, :] = jnp.zeros((POOL_HIST, POOL_WIDTH), F32)

    shift = mod_ref[3:4, :]
    scale = mod_ref[4:5, :]
    seg = seg_ref[...]
    q_scale = ATT_QK_DIM ** -0.5 * LOG2E
    u0 = 2 * QK_COLS + ATT_WIDTH
    levels = (uh_ref,) + tuple(lv_refs)

    def project(r, h):
        proj_ref[r, :] = jnp.dot(h, win_ref[...], preferred_element_type=F32)

    def finish(r):
        for which in range(2):
            gain = qkg_ref[which:which + 1, :]
            if which == 0:
                gain = gain * q_scale
            for c0 in range(0, QK_COLS, V7X_MXU_DIM):
                xc = proj_ref[r, which * QK_COLS + c0:which * QK_COLS + c0 + V7X_MXU_DIM]
                ms = _group_mean_sq(xc, seg)
                normed = xc * lax.rsqrt(ms + EPS) * gain[:, c0:c0 + V7X_MXU_DIM]
                if which == 0:
                    qt_ref[c0:c0 + V7X_MXU_DIM, r] = normed.T.astype(BF16)
                else:
                    k_ref[r, c0:c0 + V7X_MXU_DIM] = normed.astype(BF16)
        vt_ref[:, r] = proj_ref[r, 2 * QK_COLS:2 * QK_COLS + ATT_WIDTH].T.astype(BF16)

        lo0, hi = POOL_HIST + r.start, POOL_HIST + r.stop
        uh_ref[lo0:hi, :] = proj_ref[r, u0:u0 + POOL_WIDTH]
        for j in range(1, len(POOL_WINDOWS) + 1):
            lo = 8 * j if r.start == 0 else lo0
            sh, c_lo = 2 ** (j - 1), (j - 1) * POOL_GROUP_DIM
            levels[j][lo:hi, c_lo:] = levels[j - 1][lo:hi, c_lo:] + levels[j - 1][lo - sh:hi - sh, c_lo:]
        pos = t * tm + r.start + lax.broadcasted_iota(jnp.int32, (r.stop - r.start, 1), 0)
        for g, w in enumerate(POOL_WINDOWS):
            cols = slice(g * POOL_GROUP_DIM, (g + 1) * POOL_GROUP_DIM)
            cur = uh_ref[lo0:hi, cols]
            acc = levels[g + 1][lo0:hi, cols]
            cnt = jnp.minimum(pos + 1, w).astype(F32)
            dlt = (acc / cnt - cur).astype(BF16)
            y = jnp.dot(dlt, pw_ref[g], preferred_element_type=F32)
            p_ref[r, cols] = (y * ps_ref[:, cols]).astype(BF16)

    whole = slice(0, tm)
    project(whole, _modulated_norm(x_ref[...], ng_ref[...], scale, shift).astype(BF16))
    finish(whole)
    uh_ref[0:POOL_HIST, :] = uh_ref[tm:POOL_HIST + tm, :]


def _inproj(x, mod, norm_g, w_in_set, qk_gain, seg, pool_w, pool_scale, layer, e):
    b, s, d = x.shape
    tm = ROW_TILE
    w_in, w_in_idx = w_in_set
    width = w_in.shape[-1]
    row_in = pl.BlockSpec((None, tm, d), lambda bi, i: (bi, i, 0))
    row_out = pl.BlockSpec((None, tm, QK_COLS), lambda bi, i: (bi, i, 0))
    return pl.pallas_call(
        _inproj_kernel,
        grid=(b, s // tm),
        in_specs=[
            row_in,
            _mod_spec(mod, layer),
            _layer_spec(norm_g, layer, 1),
            _row_set_spec(w_in, w_in_idx, d),
            _layer_spec(qk_gain, e),
            _layer_spec(seg),
            _layer_spec(pool_w, e),
            _layer_spec(pool_scale, e),
        ],
        out_specs=[
            pl.BlockSpec((None, None, QK_COLS, tm), lambda bi, i: (bi, i, 0, 0)),
            row_out,
            pl.BlockSpec((None, None, ATT_WIDTH, tm), lambda bi, i: (bi, i, 0, 0)),
            row_out,
        ],
        out_shape=[
            jax.ShapeDtypeStruct((b, s // tm, QK_COLS, tm), BF16),
            jax.ShapeDtypeStruct((b, s, QK_COLS), BF16),
            jax.ShapeDtypeStruct((b, s // tm, ATT_WIDTH, tm), BF16),
            jax.ShapeDtypeStruct((b, s, POOL_WIDTH), BF16),
        ],
        scratch_shapes=([pltpu.VMEM((tm, width), F32)]
                        + [pltpu.VMEM((POOL_HIST + tm, POOL_WIDTH), F32)] * (1 + len(POOL_WINDOWS))),
        compiler_params=_params(2),
        name="inproj_ab",
    )(x, mod, norm_g, w_in, qk_gain, seg, pool_w, pool_scale)


def _attn_kernel(qt_ref, k_ref, vt_ref, lq1_ref, lk1_ref, lq2_ref, lk2_ref, sg_ref, *rest, lam_init, n_cast):
    cast_src = rest[:n_cast]
    o_ref = rest[n_cast]
    cast_dst = rest[n_cast + 1:2 * n_cast + 1]
    (sa0_ref, sb0_ref, sa1_ref, sb1_ref, bma0_ref, bmb0_ref, bma1_ref, bmb1_ref,
     m_ref, l_ref, acc_ref) = rest[2 * n_cast + 1:]
    for src_ref, dst_ref in zip(cast_src, cast_dst):
        dst_ref[...] = src_ref[...].astype(BF16)

    qi = pl.program_id(1)
    tq = qt_ref.shape[1]
    tk = vt_ref.shape[2]
    width = 2 * ATT_QK_DIM
    first_block_shift = jnp.where(qi == 0, 0, tk)
    lam = (jnp.exp(jnp.sum(lq1_ref[...] * lk1_ref[...], axis=-1, keepdims=True))
           - jnp.exp(jnp.sum(lq2_ref[...] * lk2_ref[...], axis=-1, keepdims=True))
           + lam_init)

    def visible_rows(c0, diagonal):
        return min(c0 + ATT_QCOLS, tk) if diagonal else tk

    def scores(h, ki, buf, mask_shift=None, diagonal=False):
        dst_ref, bm_ref = buf
        qt = qt_ref[h * width:(h + 1) * width, :]
        feat = lax.broadcasted_iota(jnp.int32, qt.shape, 0)
        zero = jnp.zeros_like(qt)
        q_maps = (jnp.where(feat < ATT_QK_DIM, qt, zero), jnp.where(feat >= ATT_QK_DIM, qt, zero))
        kb = k_ref[pl.ds(pl.multiple_of(ki * tk, tk), tk), h * width:(h + 1) * width]
        for j in range(2):
            if diagonal:
                for c0 in range(0, tq, ATT_QCOLS):
                    cols = slice(c0, c0 + ATT_QCOLS)
                    rows = visible_rows(c0, True)
                    s = jnp.dot(kb[0:rows, :], q_maps[j][:, cols], preferred_element_type=F32)
                    kpos = lax.broadcasted_iota(jnp.int32, s.shape, 0)
                    qpos = lax.broadcasted_iota(jnp.int32, s.shape, 1) + c0
                    s = jnp.where(kpos <= qpos, s, -jnp.inf)
                    dst_ref[j, 0:rows, cols] = s
                    bm_ref[j, :, cols] = jnp.max(s, axis=0, keepdims=True)
            else:
                s = jnp.dot(kb, q_maps[j], preferred_element_type=F32)
                if mask_shift is not None:
                    kpos = lax.broadcasted_iota(jnp.int32, s.shape, 0)
                    qpos = lax.broadcasted_iota(jnp.int32, s.shape, 1)
                    s = jnp.where(kpos <= qpos + mask_shift, s, -jnp.inf)
                dst_ref[j] = s
                bm_ref[j] = jnp.max(s, axis=0, keepdims=True)

    def softmax_pv(h, ki, buf, diagonal=False):
        src_ref, bm_ref = buf
        vb = vt_ref[ki, h * ATT_V_DIM:(h + 1) * ATT_V_DIM, :]
        for j in range(2):
            for c0 in range(0, tq, ATT_QCOLS):
                cols = slice(c0, c0 + ATT_QCOLS)
                rows = visible_rows(c0, diagonal)
                m_old = m_ref[j, :, cols]
                m_new = jnp.maximum(m_old, bm_ref[j, :, cols])
                alpha = jnp.exp2(m_old - m_new)
                p = jnp.exp2(src_ref[j, 0:rows, cols] - m_new)
                l_ref[j, :, cols] = alpha * l_ref[j, :, cols] + jnp.sum(p, axis=0, keepdims=True)
                acc_ref[j, :, cols] = alpha * acc_ref[j, :, cols] + jnp.dot(
                    vb[:, 0:rows], p.astype(BF16), preferred_element_type=F32)
                m_ref[j, :, cols] = m_new

    buffers = (((sa0_ref, bma0_ref), (sb0_ref, bmb0_ref)), ((sa1_ref, bma1_ref), (sb1_ref, bmb1_ref)))
    n_heads = qt_ref.shape[0] // width
    n_pairs = jnp.maximum(qi - 1, 0) // 2
    k_rest = 2 * n_pairs

    def first_scores(h):
        if h < n_heads:
            scores(h, 0, buffers[h % 2][0], mask_shift=first_block_shift)

    first_scores(0)
    for h in range(n_heads):
        buf_a, buf_b = buffers[h % 2]
        m_ref[...] = jnp.full(m_ref.shape, -jnp.inf, F32)
        l_ref[...] = jnp.zeros(l_ref.shape, F32)
        acc_ref[...] = jnp.zeros(acc_ref.shape, F32)

        @pl.when(qi == 0)
        def _(h=h, buf_a=buf_a):
            softmax_pv(h, 0, buf_a)
            first_scores(h + 1)

        def pair(t, carry, h=h, buf_a=buf_a, buf_b=buf_b):
            k0 = 2 * t
            scores(h, k0 + 1, buf_b)
            softmax_pv(h, k0, buf_a)
            scores(h, k0 + 2, buf_a)
            softmax_pv(h, k0 + 1, buf_b)
            return carry

        lax.fori_loop(0, n_pairs, pair, 0)

        @pl.when(qi % 2 == 1)
        def _(h=h, buf_a=buf_a, buf_b=buf_b):
            scores(h, qi, buf_b, diagonal=True)
            softmax_pv(h, k_rest, buf_a)
            softmax_pv(h, qi, buf_b, diagonal=True)
            first_scores(h + 1)

        @pl.when(jnp.logical_and(qi % 2 == 0, qi > 0))
        def _(h=h, buf_a=buf_a, buf_b=buf_b):
            scores(h, k_rest + 1, buf_b)
            softmax_pv(h, k_rest, buf_a)
            scores(h, qi, buf_a, diagonal=True)
            softmax_pv(h, k_rest + 1, buf_b)
            softmax_pv(h, qi, buf_a, diagonal=True)
            first_scores(h + 1)

        o = acc_ref[0] * (1.0 / l_ref[0]) - (lam * (1.0 / l_ref[1])) * acc_ref[1]
        ms = jnp.mean(o * o, axis=0, keepdims=True)
        on = (o * lax.rsqrt(ms + EPS) * sg_ref[...]) * (1.0 - lam_init)
        o_ref[:, h * ATT_V_DIM:(h + 1) * ATT_V_DIM] = on.T.astype(o_ref.dtype)


def _attention(qt, k, vt, lq1, lk1, lq2, lk2, subln_g, lam_init, e, cast_jobs):
    b, s, _ = k.shape
    nk, tk = vt.shape[1], vt.shape[3]
    tq = qt.shape[3]
    nq = s // tq
    assert tq == tk

    def cast_specs(w_rows, set_rows, first_set, n_sets):
        n_steps = n_sets * CAST_STEPS_PER_SET
        assert n_steps <= b * nq
        rows = set_rows // CAST_STEPS_PER_SET
        cols = w_rows.shape[1]
        first_blk = first_set * CAST_STEPS_PER_SET

        def chunk(bi, i):
            return jnp.minimum(bi * nq + i, n_steps - 1)

        in_spec = pl.BlockSpec((rows, cols), lambda bi, i: (first_blk + chunk(bi, i), 0))
        out_spec = pl.BlockSpec((rows, cols), lambda bi, i: (chunk(bi, i), 0))
        return in_spec, out_spec, jax.ShapeDtypeStruct((n_sets * set_rows, cols), BF16)

    cast = [cast_specs(*job) for job in cast_jobs]
    scores_buf = pltpu.VMEM((2, tk, tq), F32)
    block_max = pltpu.VMEM((2, 1, tq), F32)
    return pl.pallas_call(
        functools.partial(_attn_kernel, lam_init=lam_init, n_cast=len(cast_jobs)),
        grid=(b, nq),
        in_specs=[
            pl.BlockSpec((None, None, QK_COLS, tq), lambda bi, i: (bi, i, 0, 0)),
            pl.BlockSpec((None, s, QK_COLS), lambda bi, i: (bi, 0, 0), pipeline_mode=pl.Buffered(1)),
            pl.BlockSpec((None, nk, ATT_WIDTH, tk), lambda bi, i: (bi, 0, 0, 0),
                         pipeline_mode=pl.Buffered(1)),
            _layer_spec(lq1, e), _layer_spec(lk1, e), _layer_spec(lq2, e), _layer_spec(lk2, e),
            _layer_spec(subln_g, e),
        ] + [c[0] for c in cast],
        out_specs=[pl.BlockSpec((None, tq, ATT_WIDTH), lambda bi, i: (bi, i, 0))] + [c[1] for c in cast],
        out_shape=[jax.ShapeDtypeStruct((b, s, ATT_WIDTH), BF16)] + [c[2] for c in cast],
        scratch_shapes=[
            scores_buf, scores_buf, scores_buf, scores_buf,
            block_max, block_max, block_max, block_max,
            pltpu.VMEM((2, 1, tq), F32),
            pltpu.VMEM((2, 1, tq), F32),
            pltpu.VMEM((2, ATT_V_DIM, tq), F32),
        ],
        compiler_params=_params(2),
        name="diff_attn",
    )(qt, k, vt, lq1, lk1, lq2, lk2, subln_g, *[job[0] for job in cast_jobs])


def _conv_kernel(x_ref, mod_ref, ng_ref, win_ref, cw_ref, wout_ref, out_ref, proj_ref, vh_ref):
    t = pl.program_id(1)
    tm, d = x_ref.shape
    shift = mod_ref[3:4, :]
    scale = mod_ref[4:5, :]
    gate = mod_ref[5:6, :]

    @pl.when(t == 0)
    def _():
        vh_ref[0:CONV_HIST, :] = jnp.zeros((CONV_HIST, d), F32)

    halves = [slice(r0, r0 + tm // 2) for r0 in (0, tm // 2)]
    h = [_modulated_norm(x_ref[r, :], ng_ref[...], scale, shift).astype(BF16) for r in halves]

    def in_proj(i):
        proj_ref[halves[i], :] = jnp.dot(h[i], win_ref[...], preferred_element_type=F32)

    def gated_conv(i):
        r = halves[i]
        lo, hi = CONV_HIST + r.start, CONV_HIST + r.stop
        vh_ref[lo:hi, :] = proj_ref[r, d:2 * d] * proj_ref[r, 2 * d:3 * d]
        y = cw_ref[CONV_K - 1:CONV_K, :] * vh_ref[lo:hi, :]
        for j in range(1, CONV_K):
            y = y + cw_ref[CONV_K - 1 - j:CONV_K - j, :] * vh_ref[lo - j:hi - j, :]
        return (proj_ref[r, 0:d] * y).astype(BF16)

    def out_proj(i, z):
        r = halves[i]
        out_ref[r, :] = x_ref[r, :] + gate * jnp.dot(z, wout_ref[...], preferred_element_type=F32)

    in_proj(0)
    z0 = gated_conv(0)
    in_proj(1)
    out_proj(0, z0)
    z1 = gated_conv(1)
    out_proj(1, z1)
    vh_ref[0:CONV_HIST, :] = vh_ref[tm:tm + CONV_HIST, :]


def _conv_mixer(x, mod, norm_g, w_in_set, conv_w, w_out_set, layer, o_idx):
    b, s, d = x.shape
    tm = CONV_TILE
    (w_in, w_in_idx), (w_out, w_out_idx) = w_in_set, w_out_set
    row = pl.BlockSpec((None, tm, d), lambda bi, i: (bi, i, 0))
    return pl.pallas_call(
        _conv_kernel,
        grid=(b, s // tm),
        in_specs=[row, _mod_spec(mod, layer), _layer_spec(norm_g, layer, 1),
                  _row_set_spec(w_in, w_in_idx, d), _layer_spec(conv_w, o_idx),
                  _row_set_spec(w_out, w_out_idx, d)],
        out_specs=row,
        out_shape=jax.ShapeDtypeStruct(x.shape, F32),
        scratch_shapes=[pltpu.VMEM((tm, 3 * d), F32), pltpu.VMEM((CONV_HIST + tm, d), F32)],
        compiler_params=_params(2),
        name="conv_mixer",
    )(x, mod, norm_g, w_in, conv_w, w_out)


def kernel(x, c, norm_g, w_ada, b_ada, ffn_wg, ffn_wu, ffn_wd, w_in_ab, qk_norm_g, lambda_q1, lambda_k1,
           lambda_q2, lambda_k2, subln_g, pool_w, pool_scale, w_out_ab, w_in_c, conv_w, w_out_c):
    depth = norm_g.shape[0]
    d = x.shape[-1]
    n_even = w_in_ab.shape[0]
    mod = _ada_modulation(c, w_ada, b_ada)

    n_sets_total = 2 * depth
    n_odd = w_in_c.shape[0]
    d_ff = ffn_wg.shape[-1]

    def rows2d(w):
        return w.reshape(-1, w.shape[-1])

    ffn_sets = {0: ((ffn_wg[0, 0].astype(BF16), ffn_wu[0, 0].astype(BF16), ffn_wd[0, 0].astype(BF16)), 0)}
    w_in_ab_sets = {0: (w_in_ab[0].astype(BF16), 0)}
    w_out_ab_sets, w_in_c_sets, w_out_c_sets = {}, {}, {}
    pool_w16 = pool_w.astype(BF16)
    grp = jnp.arange(V7X_MXU_DIM) // ATT_QK_DIM
    seg = ((grp[:, None] == grp[None, :]).astype(F32) / ATT_QK_DIM).astype(BF16)
    qk_gain = jnp.tile(qk_norm_g, (1, 1, QK_COLS // ATT_QK_DIM))
    norm_g4 = norm_g.reshape(depth, 3, 1, d)
    pool_scale3 = pool_scale.reshape(n_even, 1, POOL_WIDTH)
    lam_rows = [a.reshape(n_even, 1, ATT_QK_DIM) for a in (lambda_q1, lambda_k1, lambda_q2, lambda_k2)]
    subln_col = subln_g.reshape(n_even, ATT_V_DIM, 1)

    for l in range(depth):
        x = _ffn(x, mod, norm_g4, ffn_sets[2 * l], l, 0)
        if l % 2 == 0:
            e = l // 2
            lam_init = 0.8 - 0.6 * math.exp(-0.3 * l)
            qt, k, vt, p = _inproj(x, mod, norm_g4, w_in_ab_sets[e], qk_gain, seg, pool_w16, pool_scale3, l, e)
            first_set = 2 * l + 1
            n_sets = min(4, n_sets_total - first_set)
            jobs = [(rows2d(ffn_wg), d, first_set, n_sets), (rows2d(ffn_wu), d, first_set, n_sets),
                    (rows2d(ffn_wd), d_ff, first_set, n_sets), (rows2d(w_out_ab), d, e, 1)]
            if e < n_odd:
                jobs += [(rows2d(w_in_c), d, e, 1), (rows2d(w_out_c), d, e, 1)]
            if e + 1 < n_even:
                jobs += [(rows2d(w_in_ab), d, e + 1, 1)]
            o, *cast_w = _attention(qt, k, vt, *lam_rows, subln_col, lam_init, e, jobs)
            for j in range(n_sets):
                ffn_sets[first_set + j] = (tuple(cast_w[0:3]), j)
            w_out_ab_sets[e] = (cast_w[3], 0)
            if e < n_odd:
                w_in_c_sets[e], w_out_c_sets[e] = (cast_w[4], 0), (cast_w[5], 0)
            if e + 1 < n_even:
                w_in_ab_sets[e + 1] = (cast_w[-1], 0)
            x = _ffn(x, mod, norm_g4, ffn_sets[2 * l + 1], l, 1, mixer=(o, p, w_out_ab_sets[e]))
        else:
            o_idx = l // 2
            x = _conv_mixer(x, mod, norm_g4, w_in_c_sets[o_idx], conv_w, w_out_c_sets[o_idx], l, o_idx)
            x = _ffn(x, mod, norm_g4, ffn_sets[2 * l + 1], l, 1)
    return x
```

```python
import functools
import math

import jax
import jax.numpy as jnp
from jax import lax
from jax.experimental import pallas as pl
from jax.experimental.pallas import tpu as pltpu

F32 = jnp.float32
BF16 = jnp.bfloat16

EPS = 1e-6
N_MOD = 9
ATT_HEADS = 4
ATT_QK_DIM = 64
ATT_V_DIM = 128
QK_COLS = 512
ATT_WIDTH = 512
POOL_WINDOWS = (2, 4, 8, 16)
POOL_WIDTH = 512
POOL_GROUP_DIM = 128
CONV_K = 3

V7X_MXU_DIM = 256
VMEM_LIMIT_BYTES = 56 * 1024 * 1024

ROW_TILE = 512
FFN_TILE = 1024
CONV_TILE = 1024
ATT_QCOLS = V7X_MXU_DIM
FF_CHUNK = V7X_MXU_DIM
CAST_STEPS_PER_SET = 8
POOL_HIST = 8 * len(POOL_WINDOWS)
assert POOL_WINDOWS == tuple(2 ** (g + 1) for g in range(len(POOL_WINDOWS))) and POOL_HIST >= POOL_WINDOWS[-1]
CONV_HIST = 8
LOG2E = math.log2(math.e)


def _params(n_axes):
    return pltpu.CompilerParams(
        dimension_semantics=("arbitrary",) * n_axes, vmem_limit_bytes=VMEM_LIMIT_BYTES)


def _layer_spec(arr, *lead):
    tail = arr.shape[len(lead):]
    block = (None,) * len(lead) + tail
    index = tuple(lead) + (0,) * len(tail)
    return pl.BlockSpec(block, lambda *_: index, pipeline_mode=pl.Buffered(1))


def _modulated_norm(x, g, scale, shift):
    ms = jnp.mean(x * x, axis=-1, keepdims=True)
    y = x * lax.rsqrt(ms + EPS)
    return (y * g) * (1.0 + scale) + shift


def _silu(x):
    return x * jax.nn.sigmoid(x)


def _ada_kernel(c_ref, w_ref, b_ref, o_ref):
    c_act = _silu(c_ref[...]).astype(BF16)
    w = w_ref[...].astype(BF16)
    o_ref[...] = jnp.dot(c_act, w, preferred_element_type=F32) + b_ref[...]


def _ada_modulation(c, w_ada, b_ada):
    depth, d, _ = w_ada.shape
    b = c.shape[0]
    rows = 8
    c_pad = jnp.zeros((rows, d), F32).at[:b].set(c)
    out = pl.pallas_call(
        _ada_kernel,
        grid=(depth, N_MOD),
        in_specs=[
            pl.BlockSpec((rows, d), lambda l, j: (0, 0)),
            pl.BlockSpec((None, d, d), lambda l, j: (l, 0, j)),
            pl.BlockSpec((None, None, 1, d), lambda l, j: (l, j, 0, 0)),
        ],
        out_specs=pl.BlockSpec((None, rows, d), lambda l, j: (l, 0, j)),
        out_shape=jax.ShapeDtypeStruct((depth, rows, N_MOD * d), F32),
        compiler_params=_params(2),
        name="ada_mod",
    )(c_pad, w_ada, b_ada.reshape(depth, N_MOD, 1, d))
    return out[:, :b].reshape(depth, b, N_MOD, d)


def _mod_spec(mod, layer):
    d = mod.shape[-1]
    return pl.BlockSpec((None, None, N_MOD, d), lambda bi, i: (layer, bi, 0, 0))


def _ffn_kernel(x_ref, mod_ref, ng_ref, wg_ref, wu_ref, wd_ref, *rest, sub, mix):
    if mix:
        ao_ref, po_ref, wo_ref, o_ref, a_ref = rest
        cat = jnp.concatenate([ao_ref[...], po_ref[...]], axis=-1)
        x = x_ref[...] + mod_ref[5:6, :] * jnp.dot(cat, wo_ref[...], preferred_element_type=F32)
    else:
        o_ref, a_ref = rest
        x = x_ref[...]
    shift = mod_ref[3 * sub:3 * sub + 1, :]
    scale = mod_ref[3 * sub + 1:3 * sub + 2, :]
    gate = mod_ref[3 * sub + 2:3 * sub + 3, :]
    h = _modulated_norm(x, ng_ref[...], scale, shift).astype(BF16)
    d_ff = a_ref.shape[1]
    for c0 in range(0, d_ff, FF_CHUNK):
        c1 = min(c0 + FF_CHUNK, d_ff)
        g = jnp.dot(h, wg_ref[:, c0:c1], preferred_element_type=F32)
        u = jnp.dot(h, wu_ref[:, c0:c1], preferred_element_type=F32)
        a_ref[:, c0:c1] = (_silu(g) * u).astype(BF16)
    y = jnp.dot(a_ref[...], wd_ref[...], preferred_element_type=F32)
    o_ref[...] = x + (0.5 * gate) * y


def _row_set_spec(w_rows, idx, set_rows):
    return pl.BlockSpec((set_rows, w_rows.shape[1]), lambda *_: (idx, 0), pipeline_mode=pl.Buffered(1))


def _ffn(x, mod, norm_g, weights, layer, half, mixer=None):
    b, s, d = x.shape
    (wg, wu, wd), w_idx = weights
    d_ff = wg.shape[1]
    tm = FFN_TILE
    sub = 2 * half
    row_spec = pl.BlockSpec((None, tm, d), lambda bi, i: (bi, i, 0))
    in_specs = [
        row_spec,
        _mod_spec(mod, layer),
        _layer_spec(norm_g, layer, sub),
        _row_set_spec(wg, w_idx, d),
        _row_set_spec(wu, w_idx, d),
        _row_set_spec(wd, w_idx, d_ff),
    ]
    args = [x, mod, norm_g, wg, wu, wd]
    if mixer is not None:
        ao, po, (w_out, w_out_idx) = mixer
        half_spec = pl.BlockSpec((None, tm, ATT_WIDTH), lambda bi, i: (bi, i, 0))
        in_specs += [half_spec, half_spec, _row_set_spec(w_out, w_out_idx, d)]
        args += [ao, po, w_out]
    return pl.pallas_call(
        functools.partial(_ffn_kernel, sub=sub, mix=mixer is not None),
        grid=(b, s // tm),
        in_specs=in_specs,
        out_specs=row_spec,
        out_shape=jax.ShapeDtypeStruct(x.shape, F32),
        scratch_shapes=[pltpu.VMEM((tm, d_ff), BF16)],
        compiler_params=_params(2),
        name=f"ffn{sub}" + ("_mix" if mixer is not None else ""),
    )(*args)


def _group_mean_sq(xc, seg):
    sq = xc * xc
    hi = sq.astype(BF16)
    lo = (sq - hi.astype(F32)).astype(BF16)
    return jnp.dot(hi, seg, preferred_element_type=F32) + jnp.dot(lo, seg, preferred_element_type=F32)


def _inproj_kernel(x_ref, mod_ref, ng_ref, win_ref, qkg_ref, seg_ref, pw_ref, ps_ref,
                   qt_ref, k_ref, vt_ref, p_ref, proj_ref, uh_ref, *lv_refs):
    t = pl.program_id(1)
    tm = x_ref.shape[0]

    @pl.when(t == 0)
    def _():
        uh_ref[0:POOL_HIST, :] = jnp.zeros((POOL_HIST, POOL_WIDTH), F32)

    shift = mod_ref[3:4, :]
    scale = mod_ref[4:5, :]
    seg = seg_ref[...]
    q_scale = ATT_QK_DIM ** -0.5 * LOG2E
    u0 = 2 * QK_COLS + ATT_WIDTH
    levels = (uh_ref,) + tuple(lv_refs)

    def project(r, h):
        proj_ref[r, :] = jnp.dot(h, win_ref[...], preferred_element_type=F32)

    def finish(r):
        for which in range(2):
            gain = qkg_ref[which:which + 1, :]
            if which == 0:
                gain = gain * q_scale
            for c0 in range(0, QK_COLS, V7X_MXU_DIM):
                xc = proj_ref[r, which * QK_COLS + c0:which * QK_COLS + c0 + V7X_MXU_DIM]
                ms = _group_mean_sq(xc, seg)
                normed = xc * lax.rsqrt(ms + EPS) * gain[:, c0:c0 + V7X_MXU_DIM]
                if which == 0:
                    qt_ref[c0:c0 + V7X_MXU_DIM, r] = normed.T.astype(BF16)
                else:
                    k_ref[r, c0:c0 + V7X_MXU_DIM] = normed.astype(BF16)
        vt_ref[:, r] = proj_ref[r, 2 * QK_COLS:2 * QK_COLS + ATT_WIDTH].T.astype(BF16)

        lo0, hi = POOL_HIST + r.start, POOL_HIST + r.stop
        uh_ref[lo0:hi, :] = proj_ref[r, u0:u0 + POOL_WIDTH]
        for j in range(1, len(POOL_WINDOWS) + 1):
            lo = 8 * j if r.start == 0 else lo0
            sh, c_lo = 2 ** (j - 1), (j - 1) * POOL_GROUP_DIM
            levels[j][lo:hi, c_lo:] = levels[j - 1][lo:hi, c_lo:] + levels[j - 1][lo - sh:hi - sh, c_lo:]
        pos = t * tm + r.start + lax.broadcasted_iota(jnp.int32, (r.stop - r.start, 1), 0)
        for g, w in enumerate(POOL_WINDOWS):
            cols = slice(g * POOL_GROUP_DIM, (g + 1) * POOL_GROUP_DIM)
            cur = uh_ref[lo0:hi, cols]
            acc = levels[g + 1][lo0:hi, cols]
            cnt = jnp.minimum(pos + 1, w).astype(F32)
            dlt = (acc / cnt - cur).astype(BF16)
            y = jnp.dot(dlt, pw_ref[g], preferred_element_type=F32)
            p_ref[r, cols] = (y * ps_ref[:, cols]).astype(BF16)

    whole = slice(0, tm)
    project(whole, _modulated_norm(x_ref[...], ng_ref[...], scale, shift).astype(BF16))
    finish(whole)
    uh_ref[0:POOL_HIST, :] = uh_ref[tm:POOL_HIST + tm, :]


def _inproj(x, mod, norm_g, w_in_set, qk_gain, seg, pool_w, pool_scale, layer, e):
    b, s, d = x.shape
    tm = ROW_TILE
    w_in, w_in_idx = w_in_set
    width = w_in.shape[-1]
    row_in = pl.BlockSpec((None, tm, d), lambda bi, i: (bi, i, 0))
    row_out = pl.BlockSpec((None, tm, QK_COLS), lambda bi, i: (bi, i, 0))
    return pl.pallas_call(
        _inproj_kernel,
        grid=(b, s // tm),
        in_specs=[
            row_in,
            _mod_spec(mod, layer),
            _layer_spec(norm_g, layer, 1),
            _row_set_spec(w_in, w_in_idx, d),
            _layer_spec(qk_gain, e),
            _layer_spec(seg),
            _layer_spec(pool_w, e),
            _layer_spec(pool_scale, e),
        ],
        out_specs=[
            pl.BlockSpec((None, None, QK_COLS, tm), lambda bi, i: (bi, i, 0, 0)),
            row_out,
            pl.BlockSpec((None, None, ATT_WIDTH, tm), lambda bi, i: (bi, i, 0, 0)),
            row_out,
        ],
        out_shape=[
            jax.ShapeDtypeStruct((b, s // tm, QK_COLS, tm), BF16),
            jax.ShapeDtypeStruct((b, s, QK_COLS), BF16),
            jax.ShapeDtypeStruct((b, s // tm, ATT_WIDTH, tm), BF16),
            jax.ShapeDtypeStruct((b, s, POOL_WIDTH), BF16),
        ],
        scratch_shapes=([pltpu.VMEM((tm, width), F32)]
                        + [pltpu.VMEM((POOL_HIST + tm, POOL_WIDTH), F32)] * (1 + len(POOL_WINDOWS))),
        compiler_params=_params(2),
        name="inproj_ab",
    )(x, mod, norm_g, w_in, qk_gain, seg, pool_w, pool_scale)


def _attn_kernel(qt_ref, k_ref, vt_ref, lq1_ref, lk1_ref, lq2_ref, lk2_ref, sg_ref, *rest, lam_init, n_cast):
    cast_src = rest[:n_cast]
    o_ref = rest[n_cast]
    cast_dst = rest[n_cast + 1:2 * n_cast + 1]
    (sa0_ref, sb0_ref, sa1_ref, sb1_ref, bma0_ref, bmb0_ref, bma1_ref, bmb1_ref,
     m_ref, l_ref, acc_ref) = rest[2 * n_cast + 1:]
    for src_ref, dst_ref in zip(cast_src, cast_dst):
        dst_ref[...] = src_ref[...].astype(BF16)

    qi = pl.program_id(1)
    tq = qt_ref.shape[1]
    tk = vt_ref.shape[2]
    width = 2 * ATT_QK_DIM
    first_block_shift = jnp.where(qi == 0, 0, tk)
    lam = (jnp.exp(jnp.sum(lq1_ref[...] * lk1_ref[...], axis=-1, keepdims=True))
           - jnp.exp(jnp.sum(lq2_ref[...] * lk2_ref[...], axis=-1, keepdims=True))
           + lam_init)

    def visible_rows(c0, diagonal):
        return min(c0 + ATT_QCOLS, tk) if diagonal else tk

    def scores(h, ki, buf, mask_shift=None, diagonal=False):
        dst_ref, bm_ref = buf
        qt = qt_ref[h * width:(h + 1) * width, :]
        feat = lax.broadcasted_iota(jnp.int32, qt.shape, 0)
        zero = jnp.zeros_like(qt)
        q_maps = (jnp.where(feat < ATT_QK_DIM, qt, zero), jnp.where(feat >= ATT_QK_DIM, qt, zero))
        kb = k_ref[pl.ds(pl.multiple_of(ki * tk, tk), tk), h * width:(h + 1) * width]
        for j in range(2):
            if diagonal:
                for c0 in range(0, tq, ATT_QCOLS):
                    cols = slice(c0, c0 + ATT_QCOLS)
                    rows = visible_rows(c0, True)
                    s = jnp.dot(kb[0:rows, :], q_maps[j][:, cols], preferred_element_type=F32)
                    kpos = lax.broadcasted_iota(jnp.int32, s.shape, 0)
                    qpos = lax.broadcasted_iota(jnp.int32, s.shape, 1) + c0
                    s = jnp.where(kpos <= qpos, s, -jnp.inf)
                    dst_ref[j, 0:rows, cols] = s
                    bm_ref[j, :, cols] = jnp.max(s, axis=0, keepdims=True)
            else:
                s = jnp.dot(kb, q_maps[j], preferred_element_type=F32)
                if mask_shift is not None:
                    kpos = lax.broadcasted_iota(jnp.int32, s.shape, 0)
                    qpos = lax.broadcasted_iota(jnp.int32, s.shape, 1)
                    s = jnp.where(kpos <= qpos + mask_shift, s, -jnp.inf)
                dst_ref[j] = s
                bm_ref[j] = jnp.max(s, axis=0, keepdims=True)

    def softmax_pv(h, ki, buf, diagonal=False):
        src_ref, bm_ref = buf
        vb = vt_ref[ki, h * ATT_V_DIM:(h + 1) * ATT_V_DIM, :]
        for j in range(2):
            for c0 in range(0, tq, ATT_QCOLS):
                cols = slice(c0, c0 + ATT_QCOLS)
                rows = visible_rows(c0, diagonal)
                m_old = m_ref[j, :, cols]
                m_new = jnp.maximum(m_old, bm_ref[j, :, cols])
                alpha = jnp.exp2(m_old - m_new)
                p = jnp.exp2(src_ref[j, 0:rows, cols] - m_new)
                l_ref[j, :, cols] = alpha * l_ref[j, :, cols] + jnp.sum(p, axis=0, keepdims=True)
                acc_ref[j, :, cols] = alpha * acc_ref[j, :, cols] + jnp.dot(
                    vb[:, 0:rows], p.astype(BF16), preferred_element_type=F32)
                m_ref[j, :, cols] = m_new

    buffers = (((sa0_ref, bma0_ref), (sb0_ref, bmb0_ref)), ((sa1_ref, bma1_ref), (sb1_ref, bmb1_ref)))
    n_heads = qt_ref.shape[0] // width
    n_pairs = jnp.maximum(qi - 1, 0) // 2
    k_rest = 2 * n_pairs

    def first_scores(h):
        if h < n_heads:
            scores(h, 0, buffers[h % 2][0], mask_shift=first_block_shift)

    def finish_head(h):
        o = acc_ref[0] * (1.0 / l_ref[0]) - (lam * (1.0 / l_ref[1])) * acc_ref[1]
        ms = jnp.mean(o * o, axis=0, keepdims=True)
        on = (o * lax.rsqrt(ms + EPS) * sg_ref[...]) * (1.0 - lam_init)
        o_ref[:, h * ATT_V_DIM:(h + 1) * ATT_V_DIM] = on.T.astype(o_ref.dtype)

    first_scores(0)
    for h in range(n_heads):
        buf_a, buf_b = buffers[h % 2]
        m_ref[...] = jnp.full(m_ref.shape, -jnp.inf, F32)
        l_ref[...] = jnp.zeros(l_ref.shape, F32)
        acc_ref[...] = jnp.zeros(acc_ref.shape, F32)

        @pl.when(qi == 0)
        def _(h=h, buf_a=buf_a):
            first_scores(h + 1)
            softmax_pv(h, 0, buf_a)
            finish_head(h)

        def pairs(k0, n, h=h, buf_a=buf_a, buf_b=buf_b):
            for i in range(n):
                scores(h, k0 + 2 * i + 1, buf_b)
                softmax_pv(h, k0 + 2 * i, buf_a)
                scores(h, k0 + 2 * i + 2, buf_a)
                softmax_pv(h, k0 + 2 * i + 1, buf_b)

        lax.fori_loop(0, n_pairs // 2, lambda t, c, pairs=pairs: (pairs(4 * t, 2), c)[1], 0)
        lax.fori_loop(0, n_pairs % 2, lambda t, c, pairs=pairs: (pairs(2 * (n_pairs - 1), 1), c)[1], 0)

        @pl.when(qi % 2 == 1)
        def _(h=h, buf_a=buf_a, buf_b=buf_b):
            scores(h, qi, buf_b, diagonal=True)
            softmax_pv(h, k_rest, buf_a)
            first_scores(h + 1)
            softmax_pv(h, qi, buf_b, diagonal=True)
            finish_head(h)

        @pl.when(jnp.logical_and(qi % 2 == 0, qi > 0))
        def _(h=h, buf_a=buf_a, buf_b=buf_b):
            scores(h, k_rest + 1, buf_b)
            softmax_pv(h, k_rest, buf_a)
            scores(h, qi, buf_a, diagonal=True)
            softmax_pv(h, k_rest + 1, buf_b)
            first_scores(h + 1)
            softmax_pv(h, qi, buf_a, diagonal=True)
            finish_head(h)


def _attention(qt, k, vt, lq1, lk1, lq2, lk2, subln_g, lam_init, e, cast_jobs):
    b, s, _ = k.shape
    nk, tk = vt.shape[1], vt.shape[3]
    tq = qt.shape[3]
    nq = s // tq
    assert tq == tk

    def cast_specs(w_rows, set_rows, first_set, n_sets):
        n_steps = n_sets * CAST_STEPS_PER_SET
        assert n_steps <= b * nq
        rows = set_rows // CAST_STEPS_PER_SET
        cols = w_rows.shape[1]
        first_blk = first_set * CAST_STEPS_PER_SET

        def chunk(bi, i):
            return jnp.minimum(bi * nq + i, n_steps - 1)

        in_spec = pl.BlockSpec((rows, cols), lambda bi, i: (first_blk + chunk(bi, i), 0))
        out_spec = pl.BlockSpec((rows, cols), lambda bi, i: (chunk(bi, i), 0))
        return in_spec, out_spec, jax.ShapeDtypeStruct((n_sets * set_rows, cols), BF16)

    cast = [cast_specs(*job) for job in cast_jobs]
    scores_buf = pltpu.VMEM((2, tk, tq), F32)
    block_max = pltpu.VMEM((2, 1, tq), F32)
    return pl.pallas_call(
        functools.partial(_attn_kernel, lam_init=lam_init, n_cast=len(cast_jobs)),
        grid=(b, nq),
        in_specs=[
            pl.BlockSpec((None, None, QK_COLS, tq), lambda bi, i: (bi, i, 0, 0)),
            pl.BlockSpec((None, s, QK_COLS), lambda bi, i: (bi, 0, 0), pipeline_mode=pl.Buffered(1)),
            pl.BlockSpec((None, nk, ATT_WIDTH, tk), lambda bi, i: (bi, 0, 0, 0),
                         pipeline_mode=pl.Buffered(1)),
            _layer_spec(lq1, e), _layer_spec(lk1, e), _layer_spec(lq2, e), _layer_spec(lk2, e),
            _layer_spec(subln_g, e),
        ] + [c[0] for c in cast],
        out_specs=[pl.BlockSpec((None, tq, ATT_WIDTH), lambda bi, i: (bi, i, 0))] + [c[1] for c in cast],
        out_shape=[jax.ShapeDtypeStruct((b, s, ATT_WIDTH), BF16)] + [c[2] for c in cast],
        scratch_shapes=[
            scores_buf, scores_buf, scores_buf, scores_buf,
            block_max, block_max, block_max, block_max,
            pltpu.VMEM((2, 1, tq), F32),
            pltpu.VMEM((2, 1, tq), F32),
            pltpu.VMEM((2, ATT_V_DIM, tq), F32),
        ],
        compiler_params=_params(2),
        name="diff_attn",
    )(qt, k, vt, lq1, lk1, lq2, lk2, subln_g, *[job[0] for job in cast_jobs])


def _conv_kernel(x_ref, mod_ref, ng_ref, win_ref, cw_ref, wout_ref, out_ref, proj_ref, vh_ref):
    t = pl.program_id(1)
    tm, d = x_ref.shape
    shift = mod_ref[3:4, :]
    scale = mod_ref[4:5, :]
    gate = mod_ref[5:6, :]

    @pl.when(t == 0)
    def _():
        vh_ref[0:CONV_HIST, :] = jnp.zeros((CONV_HIST, d), F32)

    halves = [slice(r0, r0 + tm // 2) for r0 in (0, tm // 2)]
    h = [_modulated_norm(x_ref[r, :], ng_ref[...], scale, shift).astype(BF16) for r in halves]

    def in_proj(i):
        proj_ref[halves[i], :] = jnp.dot(h[i], win_ref[...], preferred_element_type=F32)

    def gated_conv(i):
        r = halves[i]
        lo, hi = CONV_HIST + r.start, CONV_HIST + r.stop
        vh_ref[lo:hi, :] = proj_ref[r, d:2 * d] * proj_ref[r, 2 * d:3 * d]
        y = cw_ref[CONV_K - 1:CONV_K, :] * vh_ref[lo:hi, :]
        for j in range(1, CONV_K):
            y = y + cw_ref[CONV_K - 1 - j:CONV_K - j, :] * vh_ref[lo - j:hi - j, :]
        return (proj_ref[r, 0:d] * y).astype(BF16)

    def out_proj(i, z):
        r = halves[i]
        out_ref[r, :] = x_ref[r, :] + gate * jnp.dot(z, wout_ref[...], preferred_element_type=F32)

    in_proj(0)
    z0 = gated_conv(0)
    in_proj(1)
    out_proj(0, z0)
    z1 = gated_conv(1)
    out_proj(1, z1)
    vh_ref[0:CONV_HIST, :] = vh_ref[tm:tm + CONV_HIST, :]


def _conv_mixer(x, mod, norm_g, w_in_set, conv_w, w_out_set, layer, o_idx):
    b, s, d = x.shape
    tm = CONV_TILE
    (w_in, w_in_idx), (w_out, w_out_idx) = w_in_set, w_out_set
    row = pl.BlockSpec((None, tm, d), lambda bi, i: (bi, i, 0))
    return pl.pallas_call(
        _conv_kernel,
        grid=(b, s // tm),
        in_specs=[row, _mod_spec(mod, layer), _layer_spec(norm_g, layer, 1),
                  _row_set_spec(w_in, w_in_idx, d), _layer_spec(conv_w, o_idx),
                  _row_set_spec(w_out, w_out_idx, d)],
        out_specs=row,
        out_shape=jax.ShapeDtypeStruct(x.shape, F32),
        scratch_shapes=[pltpu.VMEM((tm, 3 * d), F32), pltpu.VMEM((CONV_HIST + tm, d), F32)],
        compiler_params=_params(2),
        name="conv_mixer",
    )(x, mod, norm_g, w_in, conv_w, w_out)


def kernel(x, c, norm_g, w_ada, b_ada, ffn_wg, ffn_wu, ffn_wd, w_in_ab, qk_norm_g, lambda_q1, lambda_k1,
           lambda_q2, lambda_k2, subln_g, pool_w, pool_scale, w_out_ab, w_in_c, conv_w, w_out_c):
    depth = norm_g.shape[0]
    d = x.shape[-1]
    n_even = w_in_ab.shape[0]
    mod = _ada_modulation(c, w_ada, b_ada)

    n_sets_total = 2 * depth
    n_odd = w_in_c.shape[0]
    d_ff = ffn_wg.shape[-1]

    def rows2d(w):
        return w.reshape(-1, w.shape[-1])

    ffn_sets = {0: ((ffn_wg[0, 0].astype(BF16), ffn_wu[0, 0].astype(BF16), ffn_wd[0, 0].astype(BF16)), 0)}
    w_in_ab_sets = {0: (w_in_ab[0].astype(BF16), 0)}
    w_out_ab_sets, w_in_c_sets, w_out_c_sets = {}, {}, {}
    pool_w16 = pool_w.astype(BF16)
    grp = jnp.arange(V7X_MXU_DIM) // ATT_QK_DIM
    seg = ((grp[:, None] == grp[None, :]).astype(F32) / ATT_QK_DIM).astype(BF16)
    qk_gain = jnp.tile(qk_norm_g, (1, 1, QK_COLS // ATT_QK_DIM))
    norm_g4 = norm_g.reshape(depth, 3, 1, d)
    pool_scale3 = pool_scale.reshape(n_even, 1, POOL_WIDTH)
    lam_rows = [a.reshape(n_even, 1, ATT_QK_DIM) for a in (lambda_q1, lambda_k1, lambda_q2, lambda_k2)]
    subln_col = subln_g.reshape(n_even, ATT_V_DIM, 1)

    for l in range(depth):
        x = _ffn(x, mod, norm_g4, ffn_sets[2 * l], l, 0)
        if l % 2 == 0:
            e = l // 2
            lam_init = 0.8 - 0.6 * math.exp(-0.3 * l)
            qt, k, vt, p = _inproj(x, mod, norm_g4, w_in_ab_sets[e], qk_gain, seg, pool_w16, pool_scale3, l, e)
            first_set = 2 * l + 1
            n_sets = min(4, n_sets_total - first_set)
            jobs = [(rows2d(ffn_wg), d, first_set, n_sets), (rows2d(ffn_wu), d, first_set, n_sets),
                    (rows2d(ffn_wd), d_ff, first_set, n_sets), (rows2d(w_out_ab), d, e, 1)]
            if e < n_odd:
                jobs += [(rows2d(w_in_c), d, e, 1), (rows2d(w_out_c), d, e, 1)]
            if e + 1 < n_even:
                jobs += [(rows2d(w_in_ab), d, e + 1, 1)]
            o, *cast_w = _attention(qt, k, vt, *lam_rows, subln_col, lam_init, e, jobs)
            for j in range(n_sets):
                ffn_sets[first_set + j] = (tuple(cast_w[0:3]), j)
            w_out_ab_sets[e] = (cast_w[3], 0)
            if e < n_odd:
                w_in_c_sets[e], w_out_c_sets[e] = (cast_w[4], 0), (cast_w[5], 0)
            if e + 1 < n_even:
                w_in_ab_sets[e + 1] = (cast_w[-1], 0)
            x = _ffn(x, mod, norm_g4, ffn_sets[2 * l + 1], l, 1, mixer=(o, p, w_out_ab_sets[e]))
        else:
            o_idx = l // 2
            x = _conv_mixer(x, mod, norm_g4, w_in_c_sets[o_idx], conv_w, w_out_c_sets[o_idx], l, o_idx)
            x = _ffn(x, mod, norm_g4, ffn_sets[2 * l + 1], l, 1)
    return x
```

```python
import functools
import math

import jax
import jax.numpy as jnp
from jax import lax
from jax.experimental import pallas as pl
from jax.experimental.pallas import tpu as pltpu

F32 = jnp.float32
BF16 = jnp.bfloat16

EPS = 1e-6
N_MOD = 9
ATT_HEADS = 4
ATT_QK_DIM = 64
ATT_V_DIM = 128
QK_COLS = 512
ATT_WIDTH = 512
POOL_WINDOWS = (2, 4, 8, 16)
POOL_WIDTH = 512
POOL_GROUP_DIM = 128
CONV_K = 3

V7X_MXU_DIM = 256
VMEM_LIMIT_BYTES = 56 * 1024 * 1024

ATT_BLOCK = 512
ROW_TILE = 1024
FFN_TILE = 1024
CONV_TILE = 1024
ATT_QCOLS = V7X_MXU_DIM
FF_CHUNK = V7X_MXU_DIM
ADA_COLS = 3072
CAST_STEPS_PER_SET = 8
POOL_HIST = 8 * len(POOL_WINDOWS)
assert POOL_WINDOWS == tuple(2 ** (g + 1) for g in range(len(POOL_WINDOWS))) and POOL_HIST >= POOL_WINDOWS[-1]
CONV_HIST = 8
LOG2E = math.log2(math.e)


def _params(n_axes):
    return pltpu.CompilerParams(
        dimension_semantics=("arbitrary",) * n_axes, vmem_limit_bytes=VMEM_LIMIT_BYTES)


def _layer_spec(arr, *lead):
    tail = arr.shape[len(lead):]
    block = (None,) * len(lead) + tail
    index = tuple(lead) + (0,) * len(tail)
    return pl.BlockSpec(block, lambda *_: index, pipeline_mode=pl.Buffered(1))


def _modulated_norm(x, g, scale, shift):
    ms = jnp.mean(x * x, axis=-1, keepdims=True)
    y = x * lax.rsqrt(ms + EPS)
    return (y * g) * (1.0 + scale) + shift


def _silu(x):
    return x * jax.nn.sigmoid(x)


def _ada_kernel(c_ref, w_ref, b_ref, o_ref):
    c_act = _silu(c_ref[...]).astype(BF16)
    w = w_ref[...].astype(BF16)
    o_ref[...] = jnp.dot(c_act, w, preferred_element_type=F32) + b_ref[...]


def _ada_modulation(c, w_ada, b_ada):
    depth, d, _ = w_ada.shape
    b = c.shape[0]
    rows = 8
    cols = ADA_COLS
    n_col_blocks = N_MOD * d // cols
    c_pad = jnp.zeros((rows, d), F32).at[:b].set(c)
    out = pl.pallas_call(
        _ada_kernel,
        grid=(depth, n_col_blocks),
        in_specs=[
            pl.BlockSpec((rows, d), lambda l, j: (0, 0)),
            pl.BlockSpec((None, d, cols), lambda l, j: (l, 0, j)),
            pl.BlockSpec((None, None, 1, cols), lambda l, j: (l, j, 0, 0)),
        ],
        out_specs=pl.BlockSpec((None, rows, cols), lambda l, j: (l, 0, j)),
        out_shape=jax.ShapeDtypeStruct((depth, rows, N_MOD * d), F32),
        compiler_params=_params(2),
        name="ada_mod",
    )(c_pad, w_ada, b_ada.reshape(depth, n_col_blocks, 1, cols))
    return out[:, :b].reshape(depth, b, N_MOD, d)


def _mod_spec(mod, layer):
    d = mod.shape[-1]
    return pl.BlockSpec((None, None, N_MOD, d), lambda bi, i: (layer, bi, 0, 0))


def _ffn_kernel(x_ref, mod_ref, ng_ref, wg_ref, wu_ref, wd_ref, *rest, sub, mix):
    if mix:
        ao_ref, po_ref, wo_ref, o_ref, a_ref = rest
        cat = jnp.concatenate([ao_ref[...], po_ref[...]], axis=-1)
        x = x_ref[...] + mod_ref[5:6, :] * jnp.dot(cat, wo_ref[...], preferred_element_type=F32)
    else:
        o_ref, a_ref = rest
        x = x_ref[...]
    shift = mod_ref[3 * sub:3 * sub + 1, :]
    scale = mod_ref[3 * sub + 1:3 * sub + 2, :]
    gate = mod_ref[3 * sub + 2:3 * sub + 3, :]
    h = _modulated_norm(x, ng_ref[...], scale, shift).astype(BF16)
    d_ff = a_ref.shape[1]
    for c0 in range(0, d_ff, FF_CHUNK):
        c1 = min(c0 + FF_CHUNK, d_ff)
        g = jnp.dot(h, wg_ref[:, c0:c1], preferred_element_type=F32)
        u = jnp.dot(h, wu_ref[:, c0:c1], preferred_element_type=F32)
        a_ref[:, c0:c1] = (_silu(g) * u).astype(BF16)
    y = jnp.dot(a_ref[...], wd_ref[...], preferred_element_type=F32)
    o_ref[...] = x + (0.5 * gate) * y


def _row_set_spec(w_rows, idx, set_rows):
    return pl.BlockSpec((set_rows, w_rows.shape[1]), lambda *_: (idx, 0), pipeline_mode=pl.Buffered(1))


def _ffn(x, mod, norm_g, weights, layer, half, mixer=None):
    b, s, d = x.shape
    (wg, wu, wd), w_idx = weights
    d_ff = wg.shape[1]
    tm = FFN_TILE
    sub = 2 * half
    row_spec = pl.BlockSpec((None, tm, d), lambda bi, i: (bi, i, 0))
    in_specs = [
        row_spec,
        _mod_spec(mod, layer),
        _layer_spec(norm_g, layer, sub),
        _row_set_spec(wg, w_idx, d),
        _row_set_spec(wu, w_idx, d),
        _row_set_spec(wd, w_idx, d_ff),
    ]
    args = [x, mod, norm_g, wg, wu, wd]
    if mixer is not None:
        ao, po, (w_out, w_out_idx) = mixer
        half_spec = pl.BlockSpec((None, tm, ATT_WIDTH), lambda bi, i: (bi, i, 0))
        in_specs += [half_spec, half_spec, _row_set_spec(w_out, w_out_idx, d)]
        args += [ao, po, w_out]
    return pl.pallas_call(
        functools.partial(_ffn_kernel, sub=sub, mix=mixer is not None),
        grid=(b, s // tm),
        in_specs=in_specs,
        out_specs=row_spec,
        out_shape=jax.ShapeDtypeStruct(x.shape, F32),
        scratch_shapes=[pltpu.VMEM((tm, d_ff), BF16)],
        compiler_params=_params(2),
        name=f"ffn{sub}" + ("_mix" if mixer is not None else ""),
    )(*args)


def _group_mean_sq(xc, seg):
    sq = xc * xc
    hi = sq.astype(BF16)
    lo = (sq - hi.astype(F32)).astype(BF16)
    return jnp.dot(hi, seg, preferred_element_type=F32) + jnp.dot(lo, seg, preferred_element_type=F32)


def _inproj_kernel(x_ref, mod_ref, ng_ref, win_ref, qkg_ref, seg_ref, pw_ref, ps_ref,
                   qt_ref, k_ref, vt_ref, p_ref, proj_ref, uh_ref, *lv_refs):
    t = pl.program_id(1)
    tm = x_ref.shape[0]

    @pl.when(t == 0)
    def _():
        uh_ref[0:POOL_HIST, :] = jnp.zeros((POOL_HIST, POOL_WIDTH), F32)

    shift = mod_ref[3:4, :]
    scale = mod_ref[4:5, :]
    seg = seg_ref[...]
    q_scale = ATT_QK_DIM ** -0.5 * LOG2E
    u0 = 2 * QK_COLS + ATT_WIDTH
    levels = (uh_ref,) + tuple(lv_refs)

    def project(r, h):
        proj_ref[r, :] = jnp.dot(h, win_ref[...], preferred_element_type=F32)

    def finish(r):
        blk = r.start // ATT_BLOCK
        for which in range(2):
            gain = qkg_ref[which:which + 1, :]
            if which == 0:
                gain = gain * q_scale
            for c0 in range(0, QK_COLS, V7X_MXU_DIM):
                xc = proj_ref[r, which * QK_COLS + c0:which * QK_COLS + c0 + V7X_MXU_DIM]
                ms = _group_mean_sq(xc, seg)
                normed = xc * lax.rsqrt(ms + EPS) * gain[:, c0:c0 + V7X_MXU_DIM]
                if which == 0:
                    qt_ref[blk, c0:c0 + V7X_MXU_DIM, :] = normed.T.astype(BF16)
                else:
                    k_ref[r, c0:c0 + V7X_MXU_DIM] = normed.astype(BF16)
        vt_ref[blk] = proj_ref[r, 2 * QK_COLS:2 * QK_COLS + ATT_WIDTH].T.astype(BF16)

        lo0, hi = POOL_HIST + r.start, POOL_HIST + r.stop
        uh_ref[lo0:hi, :] = proj_ref[r, u0:u0 + POOL_WIDTH]
        for j in range(1, len(POOL_WINDOWS) + 1):
            lo = 8 * j if r.start == 0 else lo0
            sh, c_lo = 2 ** (j - 1), (j - 1) * POOL_GROUP_DIM
            levels[j][lo:hi, c_lo:] = levels[j - 1][lo:hi, c_lo:] + levels[j - 1][lo - sh:hi - sh, c_lo:]
        pos = t * tm + r.start + lax.broadcasted_iota(jnp.int32, (r.stop - r.start, 1), 0)
        for g, w in enumerate(POOL_WINDOWS):
            cols = slice(g * POOL_GROUP_DIM, (g + 1) * POOL_GROUP_DIM)
            cur = uh_ref[lo0:hi, cols]
            acc = levels[g + 1][lo0:hi, cols]
            cnt = jnp.minimum(pos + 1, w).astype(F32)
            dlt = (acc / cnt - cur).astype(BF16)
            y = jnp.dot(dlt, pw_ref[g], preferred_element_type=F32)
            p_ref[r, cols] = (y * ps_ref[:, cols]).astype(BF16)

    project(slice(0, tm), _modulated_norm(x_ref[...], ng_ref[...], scale, shift).astype(BF16))
    for r0 in range(0, tm, ATT_BLOCK):
        finish(slice(r0, r0 + ATT_BLOCK))
    uh_ref[0:POOL_HIST, :] = uh_ref[tm:POOL_HIST + tm, :]


def _inproj(x, mod, norm_g, w_in_set, qk_gain, seg, pool_w, pool_scale, layer, e):
    b, s, d = x.shape
    tm = ROW_TILE
    blocks = tm // ATT_BLOCK
    w_in, w_in_idx = w_in_set
    width = w_in.shape[-1]
    row_in = pl.BlockSpec((None, tm, d), lambda bi, i: (bi, i, 0))
    row_out = pl.BlockSpec((None, tm, QK_COLS), lambda bi, i: (bi, i, 0))
    return pl.pallas_call(
        _inproj_kernel,
        grid=(b, s // tm),
        in_specs=[
            row_in,
            _mod_spec(mod, layer),
            _layer_spec(norm_g, layer, 1),
            _row_set_spec(w_in, w_in_idx, d),
            _layer_spec(qk_gain, e),
            _layer_spec(seg),
            _layer_spec(pool_w, e),
            _layer_spec(pool_scale, e),
        ],
        out_specs=[
            pl.BlockSpec((None, blocks, QK_COLS, ATT_BLOCK), lambda bi, i: (bi, i, 0, 0)),
            row_out,
            pl.BlockSpec((None, blocks, ATT_WIDTH, ATT_BLOCK), lambda bi, i: (bi, i, 0, 0)),
            row_out,
        ],
        out_shape=[
            jax.ShapeDtypeStruct((b, s // ATT_BLOCK, QK_COLS, ATT_BLOCK), BF16),
            jax.ShapeDtypeStruct((b, s, QK_COLS), BF16),
            jax.ShapeDtypeStruct((b, s // ATT_BLOCK, ATT_WIDTH, ATT_BLOCK), BF16),
            jax.ShapeDtypeStruct((b, s, POOL_WIDTH), BF16),
        ],
        scratch_shapes=([pltpu.VMEM((tm, width), F32)]
                        + [pltpu.VMEM((POOL_HIST + tm, POOL_WIDTH), F32)] * (1 + len(POOL_WINDOWS))),
        compiler_params=_params(2),
        name="inproj_ab",
    )(x, mod, norm_g, w_in, qk_gain, seg, pool_w, pool_scale)


def _attn_kernel(qt_ref, k_ref, vt_ref, lq1_ref, lk1_ref, lq2_ref, lk2_ref, sg_ref, *rest, lam_init, n_cast):
    cast_src = rest[:n_cast]
    o_ref = rest[n_cast]
    cast_dst = rest[n_cast + 1:2 * n_cast + 1]
    (sa0_ref, sb0_ref, sa1_ref, sb1_ref, bma0_ref, bmb0_ref, bma1_ref, bmb1_ref,
     m_ref, l_ref, acc_ref) = rest[2 * n_cast + 1:]
    for src_ref, dst_ref in zip(cast_src, cast_dst):
        dst_ref[...] = src_ref[...].astype(BF16)

    qi = pl.program_id(1)
    tq = qt_ref.shape[1]
    tk = vt_ref.shape[2]
    width = 2 * ATT_QK_DIM
    first_block_shift = jnp.where(qi == 0, 0, tk)
    lam = (jnp.exp(jnp.sum(lq1_ref[...] * lk1_ref[...], axis=-1, keepdims=True))
           - jnp.exp(jnp.sum(lq2_ref[...] * lk2_ref[...], axis=-1, keepdims=True))
           + lam_init)

    def visible_rows(c0, diagonal):
        return min(c0 + ATT_QCOLS, tk) if diagonal else tk

    def scores(h, ki, buf, mask_shift=None, diagonal=False):
        dst_ref, bm_ref = buf
        qt = qt_ref[h * width:(h + 1) * width, :]
        feat = lax.broadcasted_iota(jnp.int32, qt.shape, 0)
        zero = jnp.zeros_like(qt)
        q_maps = (jnp.where(feat < ATT_QK_DIM, qt, zero), jnp.where(feat >= ATT_QK_DIM, qt, zero))
        kb = k_ref[pl.ds(pl.multiple_of(ki * tk, tk), tk), h * width:(h + 1) * width]
        for j in range(2):
            if diagonal:
                for c0 in range(0, tq, ATT_QCOLS):
                    cols = slice(c0, c0 + ATT_QCOLS)
                    rows = visible_rows(c0, True)
                    s = jnp.dot(kb[0:rows, :], q_maps[j][:, cols], preferred_element_type=F32)
                    kpos = lax.broadcasted_iota(jnp.int32, s.shape, 0)
                    qpos = lax.broadcasted_iota(jnp.int32, s.shape, 1) + c0
                    s = jnp.where(kpos <= qpos, s, -jnp.inf)
                    dst_ref[j, 0:rows, cols] = s
                    bm_ref[j, :, cols] = jnp.max(s, axis=0, keepdims=True)
            else:
                s = jnp.dot(kb, q_maps[j], preferred_element_type=F32)
                if mask_shift is not None:
                    kpos = lax.broadcasted_iota(jnp.int32, s.shape, 0)
                    qpos = lax.broadcasted_iota(jnp.int32, s.shape, 1)
                    s = jnp.where(kpos <= qpos + mask_shift, s, -jnp.inf)
                dst_ref[j] = s
                bm_ref[j] = jnp.max(s, axis=0, keepdims=True)

    def softmax_pv(h, ki, buf, diagonal=False):
        src_ref, bm_ref = buf
        vb = vt_ref[ki, h * ATT_V_DIM:(h + 1) * ATT_V_DIM, :]
        for j in range(2):
            for c0 in range(0, tq, ATT_QCOLS):
                cols = slice(c0, c0 + ATT_QCOLS)
                rows = visible_rows(c0, diagonal)
                m_old = m_ref[j, :, cols]
                m_new = jnp.maximum(m_old, bm_ref[j, :, cols])
                alpha = jnp.exp2(m_old - m_new)
                p = jnp.exp2(src_ref[j, 0:rows, cols] - m_new)
                l_ref[j, :, cols] = alpha * l_ref[j, :, cols] + jnp.sum(p, axis=0, keepdims=True)
                acc_ref[j, :, cols] = alpha * acc_ref[j, :, cols] + jnp.dot(
                    vb[:, 0:rows], p.astype(BF16), preferred_element_type=F32)
                m_ref[j, :, cols] = m_new

    buffers = (((sa0_ref, bma0_ref), (sb0_ref, bmb0_ref)), ((sa1_ref, bma1_ref), (sb1_ref, bmb1_ref)))
    n_heads = qt_ref.shape[0] // width
    n_pairs = jnp.maximum(qi - 1, 0) // 2
    k_rest = 2 * n_pairs

    def first_scores(h):
        if h < n_heads:
            scores(h, 0, buffers[h % 2][0], mask_shift=first_block_shift)

    def finish_head(h):
        o = acc_ref[0] * (1.0 / l_ref[0]) - (lam * (1.0 / l_ref[1])) * acc_ref[1]
        ms = jnp.mean(o * o, axis=0, keepdims=True)
        on = (o * lax.rsqrt(ms + EPS) * sg_ref[...]) * (1.0 - lam_init)
        o_ref[:, h * ATT_V_DIM:(h + 1) * ATT_V_DIM] = on.T.astype(o_ref.dtype)

    first_scores(0)
    for h in range(n_heads):
        buf_a, buf_b = buffers[h % 2]
        m_ref[...] = jnp.full(m_ref.shape, -jnp.inf, F32)
        l_ref[...] = jnp.zeros(l_ref.shape, F32)
        acc_ref[...] = jnp.zeros(acc_ref.shape, F32)

        @pl.when(qi == 0)
        def _(h=h, buf_a=buf_a):
            first_scores(h + 1)
            softmax_pv(h, 0, buf_a)
            finish_head(h)

        def pairs(k0, n, h=h, buf_a=buf_a, buf_b=buf_b):
            for i in range(n):
                scores(h, k0 + 2 * i + 1, buf_b)
                softmax_pv(h, k0 + 2 * i, buf_a)
                scores(h, k0 + 2 * i + 2, buf_a)
                softmax_pv(h, k0 + 2 * i + 1, buf_b)

        lax.fori_loop(0, n_pairs // 2, lambda t, c, pairs=pairs: (pairs(4 * t, 2), c)[1], 0)
        lax.fori_loop(0, n_pairs % 2, lambda t, c, pairs=pairs: (pairs(2 * (n_pairs - 1), 1), c)[1], 0)

        @pl.when(qi % 2 == 1)
        def _(h=h, buf_a=buf_a, buf_b=buf_b):
            scores(h, qi, buf_b, diagonal=True)
            softmax_pv(h, k_rest, buf_a)
            first_scores(h + 1)
            softmax_pv(h, qi, buf_b, diagonal=True)
            finish_head(h)

        @pl.when(jnp.logical_and(qi % 2 == 0, qi > 0))
        def _(h=h, buf_a=buf_a, buf_b=buf_b):
            scores(h, k_rest + 1, buf_b)
            softmax_pv(h, k_rest, buf_a)
            scores(h, qi, buf_a, diagonal=True)
            softmax_pv(h, k_rest + 1, buf_b)
            first_scores(h + 1)
            softmax_pv(h, qi, buf_a, diagonal=True)
            finish_head(h)


def _attention(qt, k, vt, lq1, lk1, lq2, lk2, subln_g, lam_init, e, cast_jobs):
    b, s, _ = k.shape
    nk, tk = vt.shape[1], vt.shape[3]
    tq = qt.shape[3]
    nq = s // tq
    assert tq == tk

    def cast_specs(w_rows, set_rows, first_set, n_sets):
        n_steps = n_sets * CAST_STEPS_PER_SET
        assert n_steps <= b * nq
        rows = set_rows // CAST_STEPS_PER_SET
        cols = w_rows.shape[1]
        first_blk = first_set * CAST_STEPS_PER_SET

        def chunk(bi, i):
            return jnp.minimum(bi * nq + i, n_steps - 1)

        in_spec = pl.BlockSpec((rows, cols), lambda bi, i: (first_blk + chunk(bi, i), 0))
        out_spec = pl.BlockSpec((rows, cols), lambda bi, i: (chunk(bi, i), 0))
        return in_spec, out_spec, jax.ShapeDtypeStruct((n_sets * set_rows, cols), BF16)

    cast = [cast_specs(*job) for job in cast_jobs]
    scores_buf = pltpu.VMEM((2, tk, tq), F32)
    block_max = pltpu.VMEM((2, 1, tq), F32)
    return pl.pallas_call(
        functools.partial(_attn_kernel, lam_init=lam_init, n_cast=len(cast_jobs)),
        grid=(b, nq),
        in_specs=[
            pl.BlockSpec((None, None, QK_COLS, tq), lambda bi, i: (bi, i, 0, 0)),
            pl.BlockSpec((None, s, QK_COLS), lambda bi, i: (bi, 0, 0), pipeline_mode=pl.Buffered(1)),
            pl.BlockSpec((None, nk, ATT_WIDTH, tk), lambda bi, i: (bi, 0, 0, 0),
                         pipeline_mode=pl.Buffered(1)),
            _layer_spec(lq1, e), _layer_spec(lk1, e), _layer_spec(lq2, e), _layer_spec(lk2, e),
            _layer_spec(subln_g, e),
        ] + [c[0] for c in cast],
        out_specs=[pl.BlockSpec((None, tq, ATT_WIDTH), lambda bi, i: (bi, i, 0))] + [c[1] for c in cast],
        out_shape=[jax.ShapeDtypeStruct((b, s, ATT_WIDTH), BF16)] + [c[2] for c in cast],
        scratch_shapes=[
            scores_buf, scores_buf, scores_buf, scores_buf,
            block_max, block_max, block_max, block_max,
            pltpu.VMEM((2, 1, tq), F32),
            pltpu.VMEM((2, 1, tq), F32),
            pltpu.VMEM((2, ATT_V_DIM, tq), F32),
        ],
        compiler_params=_params(2),
        name="diff_attn",
    )(qt, k, vt, lq1, lk1, lq2, lk2, subln_g, *[job[0] for job in cast_jobs])


def _conv_kernel(x_ref, mod_ref, ng_ref, win_ref, cw_ref, wout_ref, out_ref, proj_ref, vh_ref):
    t = pl.program_id(1)
    tm, d = x_ref.shape
    shift = mod_ref[3:4, :]
    scale = mod_ref[4:5, :]
    gate = mod_ref[5:6, :]

    @pl.when(t == 0)
    def _():
        vh_ref[0:CONV_HIST, :] = jnp.zeros((CONV_HIST, d), F32)

    halves = [slice(r0, r0 + tm // 2) for r0 in (0, tm // 2)]
    h = [_modulated_norm(x_ref[r, :], ng_ref[...], scale, shift).astype(BF16) for r in halves]

    def in_proj(i):
        proj_ref[halves[i], :] = jnp.dot(h[i], win_ref[...], preferred_element_type=F32)

    def gated_conv(i):
        r = halves[i]
        lo, hi = CONV_HIST + r.start, CONV_HIST + r.stop
        vh_ref[lo:hi, :] = proj_ref[r, d:2 * d] * proj_ref[r, 2 * d:3 * d]
        y = cw_ref[CONV_K - 1:CONV_K, :] * vh_ref[lo:hi, :]
        for j in range(1, CONV_K):
            y = y + cw_ref[CONV_K - 1 - j:CONV_K - j, :] * vh_ref[lo - j:hi - j, :]
        return (proj_ref[r, 0:d] * y).astype(BF16)

    def out_proj(i, z):
        r = halves[i]
        out_ref[r, :] = x_ref[r, :] + gate * jnp.dot(z, wout_ref[...], preferred_element_type=F32)

    in_proj(0)
    z0 = gated_conv(0)
    in_proj(1)
    out_proj(0, z0)
    z1 = gated_conv(1)
    out_proj(1, z1)
    vh_ref[0:CONV_HIST, :] = vh_ref[tm:tm + CONV_HIST, :]


def _conv_mixer(x, mod, norm_g, w_in_set, conv_w, w_out_set, layer, o_idx):
    b, s, d = x.shape
    tm = CONV_TILE
    (w_in, w_in_idx), (w_out, w_out_idx) = w_in_set, w_out_set
    row = pl.BlockSpec((None, tm, d), lambda bi, i: (bi, i, 0))
    return pl.pallas_call(
        _conv_kernel,
        grid=(b, s // tm),
        in_specs=[row, _mod_spec(mod, layer), _layer_spec(norm_g, layer, 1),
                  _row_set_spec(w_in, w_in_idx, d), _layer_spec(conv_w, o_idx),
                  _row_set_spec(w_out, w_out_idx, d)],
        out_specs=row,
        out_shape=jax.ShapeDtypeStruct(x.shape, F32),
        scratch_shapes=[pltpu.VMEM((tm, 3 * d), F32), pltpu.VMEM((CONV_HIST + tm, d), F32)],
        compiler_params=_params(2),
        name="conv_mixer",
    )(x, mod, norm_g, w_in, conv_w, w_out)


def kernel(x, c, norm_g, w_ada, b_ada, ffn_wg, ffn_wu, ffn_wd, w_in_ab, qk_norm_g, lambda_q1, lambda_k1,
           lambda_q2, lambda_k2, subln_g, pool_w, pool_scale, w_out_ab, w_in_c, conv_w, w_out_c):
    depth = norm_g.shape[0]
    d = x.shape[-1]
    n_even = w_in_ab.shape[0]
    mod = _ada_modulation(c, w_ada, b_ada)

    n_sets_total = 2 * depth
    n_odd = w_in_c.shape[0]
    d_ff = ffn_wg.shape[-1]

    def rows2d(w):
        return w.reshape(-1, w.shape[-1])

    ffn_sets = {0: ((ffn_wg[0, 0].astype(BF16), ffn_wu[0, 0].astype(BF16), ffn_wd[0, 0].astype(BF16)), 0)}
    w_in_ab_sets = {0: (w_in_ab[0].astype(BF16), 0)}
    w_out_ab_sets, w_in_c_sets, w_out_c_sets = {}, {}, {}
    pool_w16 = pool_w.astype(BF16)
    grp = jnp.arange(V7X_MXU_DIM) // ATT_QK_DIM
    seg = ((grp[:, None] == grp[None, :]).astype(F32) / ATT_QK_DIM).astype(BF16)
    qk_gain = jnp.tile(qk_norm_g, (1, 1, QK_COLS // ATT_QK_DIM))
    norm_g4 = norm_g.reshape(depth, 3, 1, d)
    pool_scale3 = pool_scale.reshape(n_even, 1, POOL_WIDTH)
    lam_rows = [a.reshape(n_even, 1, ATT_QK_DIM) for a in (lambda_q1, lambda_k1, lambda_q2, lambda_k2)]
    subln_col = subln_g.reshape(n_even, ATT_V_DIM, 1)

    for l in range(depth):
        x = _ffn(x, mod, norm_g4, ffn_sets[2 * l], l, 0)
        if l % 2 == 0:
            e = l // 2
            lam_init = 0.8 - 0.6 * math.exp(-0.3 * l)
            qt, k, vt, p = _inproj(x, mod, norm_g4, w_in_ab_sets[e], qk_gain, seg, pool_w16, pool_scale3, l, e)
            first_set = 2 * l + 1
            n_sets = min(4, n_sets_total - first_set)
            jobs = [(rows2d(ffn_wg), d, first_set, n_sets), (rows2d(ffn_wu), d, first_set, n_sets),
                    (rows2d(ffn_wd), d_ff, first_set, n_sets), (rows2d(w_out_ab), d, e, 1)]
            if e < n_odd:
                jobs += [(rows2d(w_in_c), d, e, 1), (rows2d(w_out_c), d, e, 1)]
            if e + 1 < n_even:
                jobs += [(rows2d(w_in_ab), d, e + 1, 1)]
            o, *cast_w = _attention(qt, k, vt, *lam_rows, subln_col, lam_init, e, jobs)
            for j in range(n_sets):
                ffn_sets[first_set + j] = (tuple(cast_w[0:3]), j)
            w_out_ab_sets[e] = (cast_w[3], 0)
            if e < n_odd:
                w_in_c_sets[e], w_out_c_sets[e] = (cast_w[4], 0), (cast_w[5], 0)
            if e + 1 < n_even:
                w_in_ab_sets[e + 1] = (cast_w[-1], 0)
            x = _ffn(x, mod, norm_g4, ffn_sets[2 * l + 1], l, 1, mixer=(o, p, w_out_ab_sets[e]))
        else:
            o_idx = l // 2
            x = _conv_mixer(x, mod, norm_g4, w_in_c_sets[o_idx], conv_w, w_out_c_sets[o_idx], l, o_idx)
            x = _ffn(x, mod, norm_g4, ffn_sets[2 * l + 1], l, 1)
    return x
```

```python
import functools
import math

import jax
import jax.numpy as jnp
from jax import lax
from jax.experimental import pallas as pl
from jax.experimental.pallas import tpu as pltpu

F32 = jnp.float32
BF16 = jnp.bfloat16

EPS = 1e-6
N_MOD = 9
ATT_HEADS = 4
ATT_QK_DIM = 64
ATT_V_DIM = 128
QK_COLS = 512
ATT_WIDTH = 512
POOL_WINDOWS = (2, 4, 8, 16)
POOL_WIDTH = 512
POOL_GROUP_DIM = 128
CONV_K = 3

V7X_MXU_DIM = 256
VMEM_LIMIT_BYTES = 56 * 1024 * 1024

ATT_BLOCK = 512
ROW_TILE = 1024
FFN_TILE = 1024
CONV_TILE = 1024
ATT_QCOLS = V7X_MXU_DIM
FF_CHUNK = V7X_MXU_DIM
ADA_COLS = 3072
CAST_STEPS_PER_SET = 8
POOL_HIST = 8 * len(POOL_WINDOWS)
assert POOL_WINDOWS == tuple(2 ** (g + 1) for g in range(len(POOL_WINDOWS))) and POOL_HIST >= POOL_WINDOWS[-1]
CONV_HIST = 8
LOG2E = math.log2(math.e)


def _params(n_axes):
    return pltpu.CompilerParams(
        dimension_semantics=("arbitrary",) * n_axes, vmem_limit_bytes=VMEM_LIMIT_BYTES)


def _layer_spec(arr, *lead):
    tail = arr.shape[len(lead):]
    block = (None,) * len(lead) + tail
    index = tuple(lead) + (0,) * len(tail)
    return pl.BlockSpec(block, lambda *_: index, pipeline_mode=pl.Buffered(1))


def _modulated_norm(x, g, scale, shift):
    ms = jnp.mean(x * x, axis=-1, keepdims=True)
    y = x * lax.rsqrt(ms + EPS)
    return (y * g) * (1.0 + scale) + shift


def _silu(x):
    return x * jax.nn.sigmoid(x)


def _ada_kernel(c_ref, w_ref, b_ref, o_ref):
    c_act = _silu(c_ref[...]).astype(BF16)
    w = w_ref[...].astype(BF16)
    o_ref[...] = jnp.dot(c_act, w, preferred_element_type=F32) + b_ref[...]


def _ada_modulation(c, w_ada, b_ada):
    depth, d, _ = w_ada.shape
    b = c.shape[0]
    rows = 8
    cols = ADA_COLS
    n_col_blocks = N_MOD * d // cols
    c_pad = jnp.zeros((rows, d), F32).at[:b].set(c)
    out = pl.pallas_call(
        _ada_kernel,
        grid=(depth, n_col_blocks),
        in_specs=[
            pl.BlockSpec((rows, d), lambda l, j: (0, 0)),
            pl.BlockSpec((None, d, cols), lambda l, j: (l, 0, j)),
            pl.BlockSpec((None, None, 1, cols), lambda l, j: (l, j, 0, 0)),
        ],
        out_specs=pl.BlockSpec((None, rows, cols), lambda l, j: (l, 0, j)),
        out_shape=jax.ShapeDtypeStruct((depth, rows, N_MOD * d), F32),
        compiler_params=_params(2),
        name="ada_mod",
    )(c_pad, w_ada, b_ada.reshape(depth, n_col_blocks, 1, cols))
    return out[:, :b].reshape(depth, b, N_MOD, d)


def _mod_spec(mod, layer):
    d = mod.shape[-1]
    return pl.BlockSpec((None, None, N_MOD, d), lambda bi, i: (layer, bi, 0, 0))


def _ffn_kernel(x_ref, mod_ref, ng_ref, wg_ref, wu_ref, wd_ref, *rest, sub, mix):
    shift = mod_ref[3 * sub:3 * sub + 1, :]
    scale = mod_ref[3 * sub + 1:3 * sub + 2, :]
    gate = mod_ref[3 * sub + 2:3 * sub + 3, :]
    if mix:
        ao_ref, po_ref, wo_ref, o_ref, a_ref = rest
        tm = x_ref.shape[0]
        xs, hs = [], []
        for r in (slice(0, tm // 2), slice(tm // 2, tm)):
            cat = jnp.concatenate([ao_ref[r, :], po_ref[r, :]], axis=-1)
            xr = x_ref[r, :] + mod_ref[5:6, :] * jnp.dot(cat, wo_ref[...], preferred_element_type=F32)
            xs.append(xr)
            hs.append(_modulated_norm(xr, ng_ref[...], scale, shift).astype(BF16))
        x = jnp.concatenate(xs, axis=0)
        h = jnp.concatenate(hs, axis=0)
    else:
        o_ref, a_ref = rest
        x = x_ref[...]
        h = _modulated_norm(x, ng_ref[...], scale, shift).astype(BF16)
    d_ff = a_ref.shape[1]
    for c0 in range(0, d_ff, FF_CHUNK):
        c1 = min(c0 + FF_CHUNK, d_ff)
        g = jnp.dot(h, wg_ref[:, c0:c1], preferred_element_type=F32)
        u = jnp.dot(h, wu_ref[:, c0:c1], preferred_element_type=F32)
        a_ref[:, c0:c1] = (_silu(g) * u).astype(BF16)
    y = jnp.dot(a_ref[...], wd_ref[...], preferred_element_type=F32)
    o_ref[...] = x + (0.5 * gate) * y


def _row_set_spec(w_rows, idx, set_rows):
    return pl.BlockSpec((set_rows, w_rows.shape[1]), lambda *_: (idx, 0), pipeline_mode=pl.Buffered(1))


def _ffn(x, mod, norm_g, weights, layer, half, mixer=None):
    b, s, d = x.shape
    (wg, wu, wd), w_idx = weights
    d_ff = wg.shape[1]
    tm = FFN_TILE
    sub = 2 * half
    row_spec = pl.BlockSpec((None, tm, d), lambda bi, i: (bi, i, 0))
    in_specs = [
        row_spec,
        _mod_spec(mod, layer),
        _layer_spec(norm_g, layer, sub),
        _row_set_spec(wg, w_idx, d),
        _row_set_spec(wu, w_idx, d),
        _row_set_spec(wd, w_idx, d_ff),
    ]
    args = [x, mod, norm_g, wg, wu, wd]
    if mixer is not None:
        ao, po, (w_out, w_out_idx) = mixer
        half_spec = pl.BlockSpec((None, tm, ATT_WIDTH), lambda bi, i: (bi, i, 0))
        in_specs += [half_spec, half_spec, _row_set_spec(w_out, w_out_idx, d)]
        args += [ao, po, w_out]
    return pl.pallas_call(
        functools.partial(_ffn_kernel, sub=sub, mix=mixer is not None),
        grid=(b, s // tm),
        in_specs=in_specs,
        out_specs=row_spec,
        out_shape=jax.ShapeDtypeStruct(x.shape, F32),
        scratch_shapes=[pltpu.VMEM((tm, d_ff), BF16)],
        compiler_params=_params(2),
        name=f"ffn{sub}" + ("_mix" if mixer is not None else ""),
    )(*args)


def _group_mean_sq(xc, seg):
    sq = xc * xc
    hi = sq.astype(BF16)
    lo = (sq - hi.astype(F32)).astype(BF16)
    return jnp.dot(hi, seg, preferred_element_type=F32) + jnp.dot(lo, seg, preferred_element_type=F32)


def _inproj_kernel(x_ref, mod_ref, ng_ref, win_ref, qkg_ref, seg_ref, pw_ref, ps_ref,
                   qt_ref, k_ref, vt_ref, p_ref, proj_ref, uh_ref, *lv_refs):
    t = pl.program_id(1)
    tm = x_ref.shape[0]

    @pl.when(t == 0)
    def _():
        uh_ref[0:POOL_HIST, :] = jnp.zeros((POOL_HIST, POOL_WIDTH), F32)

    shift = mod_ref[3:4, :]
    scale = mod_ref[4:5, :]
    seg = seg_ref[...]
    q_scale = ATT_QK_DIM ** -0.5 * LOG2E
    u0 = 2 * QK_COLS + ATT_WIDTH
    levels = (uh_ref,) + tuple(lv_refs)

    def project(r, h):
        proj_ref[r, :] = jnp.dot(h, win_ref[...], preferred_element_type=F32)

    def finish(r):
        blk = r.start // ATT_BLOCK
        for which in range(2):
            gain = qkg_ref[which:which + 1, :]
            if which == 0:
                gain = gain * q_scale
            for c0 in range(0, QK_COLS, V7X_MXU_DIM):
                xc = proj_ref[r, which * QK_COLS + c0:which * QK_COLS + c0 + V7X_MXU_DIM]
                ms = _group_mean_sq(xc, seg)
                normed = xc * lax.rsqrt(ms + EPS) * gain[:, c0:c0 + V7X_MXU_DIM]
                if which == 0:
                    qt_ref[blk, c0:c0 + V7X_MXU_DIM, :] = normed.T.astype(BF16)
                else:
                    k_ref[r, c0:c0 + V7X_MXU_DIM] = normed.astype(BF16)
        vt_ref[blk] = proj_ref[r, 2 * QK_COLS:2 * QK_COLS + ATT_WIDTH].T.astype(BF16)

        lo0, hi = POOL_HIST + r.start, POOL_HIST + r.stop
        uh_ref[lo0:hi, :] = proj_ref[r, u0:u0 + POOL_WIDTH]
        for j in range(1, len(POOL_WINDOWS) + 1):
            lo = 8 * j if r.start == 0 else lo0
            sh, c_lo = 2 ** (j - 1), (j - 1) * POOL_GROUP_DIM
            levels[j][lo:hi, c_lo:] = levels[j - 1][lo:hi, c_lo:] + levels[j - 1][lo - sh:hi - sh, c_lo:]
        pos = t * tm + r.start + lax.broadcasted_iota(jnp.int32, (r.stop - r.start, 1), 0)
        for g, w in enumerate(POOL_WINDOWS):
            cols = slice(g * POOL_GROUP_DIM, (g + 1) * POOL_GROUP_DIM)
            cur = uh_ref[lo0:hi, cols]
            acc = levels[g + 1][lo0:hi, cols]
            cnt = jnp.minimum(pos + 1, w).astype(F32)
            dlt = (acc / cnt - cur).astype(BF16)
            y = jnp.dot(dlt, pw_ref[g], preferred_element_type=F32)
            p_ref[r, cols] = (y * ps_ref[:, cols]).astype(BF16)

    project(slice(0, tm), _modulated_norm(x_ref[...], ng_ref[...], scale, shift).astype(BF16))
    for r0 in range(0, tm, ATT_BLOCK):
        finish(slice(r0, r0 + ATT_BLOCK))
    uh_ref[0:POOL_HIST, :] = uh_ref[tm:POOL_HIST + tm, :]


def _inproj(x, mod, norm_g, w_in_set, qk_gain, seg, pool_w, pool_scale, layer, e):
    b, s, d = x.shape
    tm = ROW_TILE
    blocks = tm // ATT_BLOCK
    w_in, w_in_idx = w_in_set
    width = w_in.shape[-1]
    row_in = pl.BlockSpec((None, tm, d), lambda bi, i: (bi, i, 0))
    row_out = pl.BlockSpec((None, tm, QK_COLS), lambda bi, i: (bi, i, 0))
    return pl.pallas_call(
        _inproj_kernel,
        grid=(b, s // tm),
        in_specs=[
            row_in,
            _mod_spec(mod, layer),
            _layer_spec(norm_g, layer, 1),
            _row_set_spec(w_in, w_in_idx, d),
            _layer_spec(qk_gain, e),
            _layer_spec(seg),
            _layer_spec(pool_w, e),
            _layer_spec(pool_scale, e),
        ],
        out_specs=[
            pl.BlockSpec((None, blocks, QK_COLS, ATT_BLOCK), lambda bi, i: (bi, i, 0, 0)),
            row_out,
            pl.BlockSpec((None, blocks, ATT_WIDTH, ATT_BLOCK), lambda bi, i: (bi, i, 0, 0)),
            row_out,
        ],
        out_shape=[
            jax.ShapeDtypeStruct((b, s // ATT_BLOCK, QK_COLS, ATT_BLOCK), BF16),
            jax.ShapeDtypeStruct((b, s, QK_COLS), BF16),
            jax.ShapeDtypeStruct((b, s // ATT_BLOCK, ATT_WIDTH, ATT_BLOCK), BF16),
            jax.ShapeDtypeStruct((b, s, POOL_WIDTH), BF16),
        ],
        scratch_shapes=([pltpu.VMEM((tm, width), F32)]
                        + [pltpu.VMEM((POOL_HIST + tm, POOL_WIDTH), F32)] * (1 + len(POOL_WINDOWS))),
        compiler_params=_params(2),
        name="inproj_ab",
    )(x, mod, norm_g, w_in, qk_gain, seg, pool_w, pool_scale)


def _attn_kernel(qt_ref, k_ref, vt_ref, lq1_ref, lk1_ref, lq2_ref, lk2_ref, sg_ref, *rest, lam_init, n_cast):
    cast_src = rest[:n_cast]
    o_ref = rest[n_cast]
    cast_dst = rest[n_cast + 1:2 * n_cast + 1]
    (sa0_ref, sb0_ref, sa1_ref, sb1_ref, bma0_ref, bmb0_ref, bma1_ref, bmb1_ref,
     m_ref, l_ref, acc_ref) = rest[2 * n_cast + 1:]
    for src_ref, dst_ref in zip(cast_src, cast_dst):
        dst_ref[...] = src_ref[...].astype(BF16)

    qi = pl.program_id(1)
    tq = qt_ref.shape[1]
    tk = vt_ref.shape[2]
    width = 2 * ATT_QK_DIM
    first_block_shift = jnp.where(qi == 0, 0, tk)
    lam = (jnp.exp(jnp.sum(lq1_ref[...] * lk1_ref[...], axis=-1, keepdims=True))
           - jnp.exp(jnp.sum(lq2_ref[...] * lk2_ref[...], axis=-1, keepdims=True))
           + lam_init)

    def visible_rows(c0, diagonal):
        return min(c0 + ATT_QCOLS, tk) if diagonal else tk

    def scores(h, ki, buf, mask_shift=None, diagonal=False):
        dst_ref, bm_ref = buf
        qt = qt_ref[h * width:(h + 1) * width, :]
        feat = lax.broadcasted_iota(jnp.int32, qt.shape, 0)
        zero = jnp.zeros_like(qt)
        q_maps = (jnp.where(feat < ATT_QK_DIM, qt, zero), jnp.where(feat >= ATT_QK_DIM, qt, zero))
        kb = k_ref[pl.ds(pl.multiple_of(ki * tk, tk), tk), h * width:(h + 1) * width]
        for j in range(2):
            if diagonal:
                for c0 in range(0, tq, ATT_QCOLS):
                    cols = slice(c0, c0 + ATT_QCOLS)
                    rows = visible_rows(c0, True)
                    s = jnp.dot(kb[0:rows, :], q_maps[j][:, cols], preferred_element_type=F32)
                    kpos = lax.broadcasted_iota(jnp.int32, s.shape, 0)
                    qpos = lax.broadcasted_iota(jnp.int32, s.shape, 1) + c0
                    s = jnp.where(kpos <= qpos, s, -jnp.inf)
                    dst_ref[j, 0:rows, cols] = s
                    bm_ref[j, :, cols] = jnp.max(s, axis=0, keepdims=True)
            else:
                s = jnp.dot(kb, q_maps[j], preferred_element_type=F32)
                if mask_shift is not None:
                    kpos = lax.broadcasted_iota(jnp.int32, s.shape, 0)
                    qpos = lax.broadcasted_iota(jnp.int32, s.shape, 1)
                    s = jnp.where(kpos <= qpos + mask_shift, s, -jnp.inf)
                dst_ref[j] = s
                bm_ref[j] = jnp.max(s, axis=0, keepdims=True)

    def softmax_pv(h, ki, buf, diagonal=False):
        src_ref, bm_ref = buf
        vb = vt_ref[ki, h * ATT_V_DIM:(h + 1) * ATT_V_DIM, :]
        for j in range(2):
            for c0 in range(0, tq, ATT_QCOLS):
                cols = slice(c0, c0 + ATT_QCOLS)
                rows = visible_rows(c0, diagonal)
                m_old = m_ref[j, :, cols]
                m_new = jnp.maximum(m_old, bm_ref[j, :, cols])
                alpha = jnp.exp2(m_old - m_new)
                p = jnp.exp2(src_ref[j, 0:rows, cols] - m_new)
                l_ref[j, :, cols] = alpha * l_ref[j, :, cols] + jnp.sum(p, axis=0, keepdims=True)
                acc_ref[j, :, cols] = alpha * acc_ref[j, :, cols] + jnp.dot(
                    vb[:, 0:rows], p.astype(BF16), preferred_element_type=F32)
                m_ref[j, :, cols] = m_new

    buffers = (((sa0_ref, bma0_ref), (sb0_ref, bmb0_ref)), ((sa1_ref, bma1_ref), (sb1_ref, bmb1_ref)))
    n_heads = qt_ref.shape[0] // width
    n_pairs = jnp.maximum(qi - 1, 0) // 2
    k_rest = 2 * n_pairs

    def first_scores(h):
        if h < n_heads:
            scores(h, 0, buffers[h % 2][0], mask_shift=first_block_shift)

    def finish_head(h):
        o = acc_ref[0] * (1.0 / l_ref[0]) - (lam * (1.0 / l_ref[1])) * acc_ref[1]
        ms = jnp.mean(o * o, axis=0, keepdims=True)
        on = (o * lax.rsqrt(ms + EPS) * sg_ref[...]) * (1.0 - lam_init)
        o_ref[:, h * ATT_V_DIM:(h + 1) * ATT_V_DIM] = on.T.astype(o_ref.dtype)

    first_scores(0)
    for h in range(n_heads):
        buf_a, buf_b = buffers[h % 2]
        m_ref[...] = jnp.full(m_ref.shape, -jnp.inf, F32)
        l_ref[...] = jnp.zeros(l_ref.shape, F32)
        acc_ref[...] = jnp.zeros(acc_ref.shape, F32)

        @pl.when(qi == 0)
        def _(h=h, buf_a=buf_a):
            first_scores(h + 1)
            softmax_pv(h, 0, buf_a)
            finish_head(h)

        def pairs(k0, n, h=h, buf_a=buf_a, buf_b=buf_b):
            for i in range(n):
                scores(h, k0 + 2 * i + 1, buf_b)
                softmax_pv(h, k0 + 2 * i, buf_a)
                scores(h, k0 + 2 * i + 2, buf_a)
                softmax_pv(h, k0 + 2 * i + 1, buf_b)

        lax.fori_loop(0, n_pairs // 2, lambda t, c, pairs=pairs: (pairs(4 * t, 2), c)[1], 0)
        lax.fori_loop(0, n_pairs % 2, lambda t, c, pairs=pairs: (pairs(2 * (n_pairs - 1), 1), c)[1], 0)

        @pl.when(qi % 2 == 1)
        def _(h=h, buf_a=buf_a, buf_b=buf_b):
            scores(h, qi, buf_b, diagonal=True)
            softmax_pv(h, k_rest, buf_a)
            first_scores(h + 1)
            softmax_pv(h, qi, buf_b, diagonal=True)
            finish_head(h)

        @pl.when(jnp.logical_and(qi % 2 == 0, qi > 0))
        def _(h=h, buf_a=buf_a, buf_b=buf_b):
            scores(h, k_rest + 1, buf_b)
            softmax_pv(h, k_rest, buf_a)
            scores(h, qi, buf_a, diagonal=True)
            softmax_pv(h, k_rest + 1, buf_b)
            first_scores(h + 1)
            softmax_pv(h, qi, buf_a, diagonal=True)
            finish_head(h)


def _attention(qt, k, vt, lq1, lk1, lq2, lk2, subln_g, lam_init, e, cast_jobs):
    b, s, _ = k.shape
    nk, tk = vt.shape[1], vt.shape[3]
    tq = qt.shape[3]
    nq = s // tq
    assert tq == tk

    def cast_specs(w_rows, set_rows, first_set, n_sets):
        n_steps = n_sets * CAST_STEPS_PER_SET
        assert n_steps <= b * nq
        rows = set_rows // CAST_STEPS_PER_SET
        cols = w_rows.shape[1]
        first_blk = first_set * CAST_STEPS_PER_SET

        def chunk(bi, i):
            return jnp.minimum(bi * nq + i, n_steps - 1)

        in_spec = pl.BlockSpec((rows, cols), lambda bi, i: (first_blk + chunk(bi, i), 0))
        out_spec = pl.BlockSpec((rows, cols), lambda bi, i: (chunk(bi, i), 0))
        return in_spec, out_spec, jax.ShapeDtypeStruct((n_sets * set_rows, cols), BF16)

    cast = [cast_specs(*job) for job in cast_jobs]
    scores_buf = pltpu.VMEM((2, tk, tq), F32)
    block_max = pltpu.VMEM((2, 1, tq), F32)
    return pl.pallas_call(
        functools.partial(_attn_kernel, lam_init=lam_init, n_cast=len(cast_jobs)),
        grid=(b, nq),
        in_specs=[
            pl.BlockSpec((None, None, QK_COLS, tq), lambda bi, i: (bi, i, 0, 0)),
            pl.BlockSpec((None, s, QK_COLS), lambda bi, i: (bi, 0, 0), pipeline_mode=pl.Buffered(1)),
            pl.BlockSpec((None, nk, ATT_WIDTH, tk), lambda bi, i: (bi, 0, 0, 0),
                         pipeline_mode=pl.Buffered(1)),
            _layer_spec(lq1, e), _layer_spec(lk1, e), _layer_spec(lq2, e), _layer_spec(lk2, e),
            _layer_spec(subln_g, e),
        ] + [c[0] for c in cast],
        out_specs=[pl.BlockSpec((None, tq, ATT_WIDTH), lambda bi, i: (bi, i, 0))] + [c[1] for c in cast],
        out_shape=[jax.ShapeDtypeStruct((b, s, ATT_WIDTH), BF16)] + [c[2] for c in cast],
        scratch_shapes=[
            scores_buf, scores_buf, scores_buf, scores_buf,
            block_max, block_max, block_max, block_max,
            pltpu.VMEM((2, 1, tq), F32),
            pltpu.VMEM((2, 1, tq), F32),
            pltpu.VMEM((2, ATT_V_DIM, tq), F32),
        ],
        compiler_params=_params(2),
        name="diff_attn",
    )(qt, k, vt, lq1, lk1, lq2, lk2, subln_g, *[job[0] for job in cast_jobs])


def _conv_kernel(x_ref, mod_ref, ng_ref, win_ref, cw_ref, wout_ref, out_ref, proj_ref, vh_ref):
    t = pl.program_id(1)
    tm, d = x_ref.shape
    shift = mod_ref[3:4, :]
    scale = mod_ref[4:5, :]
    gate = mod_ref[5:6, :]

    @pl.when(t == 0)
    def _():
        vh_ref[0:CONV_HIST, :] = jnp.zeros((CONV_HIST, d), F32)

    halves = [slice(r0, r0 + tm // 2) for r0 in (0, tm // 2)]
    h = [_modulated_norm(x_ref[r, :], ng_ref[...], scale, shift).astype(BF16) for r in halves]

    def in_proj(i):
        proj_ref[halves[i], :] = jnp.dot(h[i], win_ref[...], preferred_element_type=F32)

    def gated_conv(i):
        r = halves[i]
        lo, hi = CONV_HIST + r.start, CONV_HIST + r.stop
        vh_ref[lo:hi, :] = proj_ref[r, d:2 * d] * proj_ref[r, 2 * d:3 * d]
        y = cw_ref[CONV_K - 1:CONV_K, :] * vh_ref[lo:hi, :]
        for j in range(1, CONV_K):
            y = y + cw_ref[CONV_K - 1 - j:CONV_K - j, :] * vh_ref[lo - j:hi - j, :]
        return (proj_ref[r, 0:d] * y).astype(BF16)

    def out_proj(i, z):
        r = halves[i]
        out_ref[r, :] = x_ref[r, :] + gate * jnp.dot(z, wout_ref[...], preferred_element_type=F32)

    in_proj(0)
    z0 = gated_conv(0)
    in_proj(1)
    out_proj(0, z0)
    z1 = gated_conv(1)
    out_proj(1, z1)
    vh_ref[0:CONV_HIST, :] = vh_ref[tm:tm + CONV_HIST, :]


def _conv_mixer(x, mod, norm_g, w_in_set, conv_w, w_out_set, layer, o_idx):
    b, s, d = x.shape
    tm = CONV_TILE
    (w_in, w_in_idx), (w_out, w_out_idx) = w_in_set, w_out_set
    row = pl.BlockSpec((None, tm, d), lambda bi, i: (bi, i, 0))
    return pl.pallas_call(
        _conv_kernel,
        grid=(b, s // tm),
        in_specs=[row, _mod_spec(mod, layer), _layer_spec(norm_g, layer, 1),
                  _row_set_spec(w_in, w_in_idx, d), _layer_spec(conv_w, o_idx),
                  _row_set_spec(w_out, w_out_idx, d)],
        out_specs=row,
        out_shape=jax.ShapeDtypeStruct(x.shape, F32),
        scratch_shapes=[pltpu.VMEM((tm, 3 * d), F32), pltpu.VMEM((CONV_HIST + tm, d), F32)],
        compiler_params=_params(2),
        name="conv_mixer",
    )(x, mod, norm_g, w_in, conv_w, w_out)


def kernel(x, c, norm_g, w_ada, b_ada, ffn_wg, ffn_wu, ffn_wd, w_in_ab, qk_norm_g, lambda_q1, lambda_k1,
           lambda_q2, lambda_k2, subln_g, pool_w, pool_scale, w_out_ab, w_in_c, conv_w, w_out_c):
    depth = norm_g.shape[0]
    d = x.shape[-1]
    n_even = w_in_ab.shape[0]
    mod = _ada_modulation(c, w_ada, b_ada)

    n_sets_total = 2 * depth
    n_odd = w_in_c.shape[0]
    d_ff = ffn_wg.shape[-1]

    def rows2d(w):
        return w.reshape(-1, w.shape[-1])

    ffn_sets = {0: ((ffn_wg[0, 0].astype(BF16), ffn_wu[0, 0].astype(BF16), ffn_wd[0, 0].astype(BF16)), 0)}
    w_in_ab_sets = {0: (w_in_ab[0].astype(BF16), 0)}
    w_out_ab_sets, w_in_c_sets, w_out_c_sets = {}, {}, {}
    pool_w16 = pool_w.astype(BF16)
    grp = jnp.arange(V7X_MXU_DIM) // ATT_QK_DIM
    seg = ((grp[:, None] == grp[None, :]).astype(F32) / ATT_QK_DIM).astype(BF16)
    qk_gain = jnp.tile(qk_norm_g, (1, 1, QK_COLS // ATT_QK_DIM))
    norm_g4 = norm_g.reshape(depth, 3, 1, d)
    pool_scale3 = pool_scale.reshape(n_even, 1, POOL_WIDTH)
    lam_rows = [a.reshape(n_even, 1, ATT_QK_DIM) for a in (lambda_q1, lambda_k1, lambda_q2, lambda_k2)]
    subln_col = subln_g.reshape(n_even, ATT_V_DIM, 1)

    for l in range(depth):
        x = _ffn(x, mod, norm_g4, ffn_sets[2 * l], l, 0)
        if l % 2 == 0:
            e = l // 2
            lam_init = 0.8 - 0.6 * math.exp(-0.3 * l)
            qt, k, vt, p = _inproj(x, mod, norm_g4, w_in_ab_sets[e], qk_gain, seg, pool_w16, pool_scale3, l, e)
            first_set = 2 * l + 1
            n_sets = min(4, n_sets_total - first_set)
            jobs = [(rows2d(ffn_wg), d, first_set, n_sets), (rows2d(ffn_wu), d, first_set, n_sets),
                    (rows2d(ffn_wd), d_ff, first_set, n_sets), (rows2d(w_out_ab), d, e, 1)]
            if e < n_odd:
                jobs += [(rows2d(w_in_c), d, e, 1), (rows2d(w_out_c), d, e, 1)]
            if e + 1 < n_even:
                jobs += [(rows2d(w_in_ab), d, e + 1, 1)]
            o, *cast_w = _attention(qt, k, vt, *lam_rows, subln_col, lam_init, e, jobs)
            for j in range(n_sets):
                ffn_sets[first_set + j] = (tuple(cast_w[0:3]), j)
            w_out_ab_sets[e] = (cast_w[3], 0)
            if e < n_odd:
                w_in_c_sets[e], w_out_c_sets[e] = (cast_w[4], 0), (cast_w[5], 0)
            if e + 1 < n_even:
                w_in_ab_sets[e + 1] = (cast_w[-1], 0)
            x = _ffn(x, mod, norm_g4, ffn_sets[2 * l + 1], l, 1, mixer=(o, p, w_out_ab_sets[e]))
        else:
            o_idx = l // 2
            x = _conv_mixer(x, mod, norm_g4, w_in_c_sets[o_idx], conv_w, w_out_c_sets[o_idx], l, o_idx)
            x = _ffn(x, mod, norm_g4, ffn_sets[2 * l + 1], l, 1)
    return x
```
